```python
import jax, jax.numpy as jnp
from jax import lax
import numpy as np

D_MODEL = 2048
BATCH = 4
SEQ = 4096
DEPTH = 4

MEM_LEN = 256
N_MIXERS = 2
N_GLA_LAYERS = (DEPTH + 1) // 2
N_SWA_LAYERS = DEPTH // 2
RMS_EPS = 1e-5
NEG_INF = -1e30

X_HEADS = 4
X_HEAD_DIM = D_MODEL // 16
X_WIDTH = X_HEADS * X_HEAD_DIM

MIX_WIDTH = D_MODEL

GLA_HEADS = 4
GLA_DK = D_MODEL // 8
GLA_DV = (MIX_WIDTH - X_WIDTH) // GLA_HEADS
GLA_K = GLA_HEADS * GLA_DK
GLA_V = GLA_HEADS * GLA_DV
GLA_GATE_RANK = 16
GLA_TAU = 16.0
GLA_CHUNK = 64
GLA_IN = 2 * GLA_K + 2 * GLA_V + GLA_GATE_RANK + X_WIDTH

SWA_HEAD_DIM = 64
SWA_Q_HEADS = (MIX_WIDTH - X_WIDTH) // SWA_HEAD_DIM
SWA_GROUP = 8
SWA_KV_HEADS = SWA_Q_HEADS // SWA_GROUP
SWA_WINDOW = 128
SWA_BLOCK = SWA_WINDOW
SWA_IN = SWA_Q_HEADS * SWA_HEAD_DIM + 2 * SWA_KV_HEADS * SWA_HEAD_DIM + X_WIDTH

D_FF = 4 * D_MODEL

kernel_name = 'hybrid_gla_swa_sink_alibi_memxattn_sqrelu'


def rmsnorm(x, g):
    xf = x.astype(jnp.float32)
    y = xf * lax.rsqrt(jnp.mean(xf * xf, axis=-1, keepdims=True) + RMS_EPS)
    return (y * g.astype(jnp.float32)).astype(x.dtype)


def alibi_slopes(n):
    return jnp.exp2(-8.0 * jnp.arange(1, n + 1, dtype=jnp.float32) / n)


def gla_mixer(q, k, v, gate_lr, w_gate_up, b_gate):
    dt = v.dtype
    B, S = q.shape[0], q.shape[1]
    C = GLA_CHUNK
    n = S // C
    pre = (gate_lr @ w_gate_up + b_gate).astype(jnp.float32)
    log_a = jax.nn.log_sigmoid(pre) / GLA_TAU
    log_a = log_a.reshape(B, n, C, GLA_HEADS, GLA_DK)
    qc = q.astype(jnp.float32).reshape(B, n, C, GLA_HEADS, GLA_DK) * (GLA_DK ** -0.5)
    kc = k.astype(jnp.float32).reshape(B, n, C, GLA_HEADS, GLA_DK)
    vc = v.astype(jnp.float32).reshape(B, n, C, GLA_HEADS, GLA_DV)
    b = jnp.cumsum(log_a, axis=2)
    b_last = b[:, :, -1:]
    q_in = qc * jnp.exp(b)
    k_in = kc * jnp.exp(-b)
    k_out = kc * jnp.exp(b_last - b)
    causal = jnp.tril(jnp.ones((C, C), dtype=bool))
    A = jnp.einsum('bnihd,bnjhd->bnhij', q_in, k_in)
    A = jnp.where(causal, A, 0.0)
    o_intra = jnp.einsum('bnhij,bnjhv->bnihv', A, vc)

    def step(state, xs):
        q_n, k_n, v_n, decay_n = xs
        o_n = jnp.einsum('bihd,bhdv->bihv', q_n, state)
        state = decay_n[..., None] * state + jnp.einsum('bjhd,bjhv->bhdv', k_n, v_n)
        return state, o_n

    xs = (jnp.moveaxis(q_in, 1, 0), jnp.moveaxis(k_out, 1, 0), jnp.moveaxis(vc, 1, 0),
          jnp.moveaxis(jnp.exp(b_last[:, :, 0]), 1, 0))
    state0 = jnp.zeros((B, GLA_HEADS, GLA_DK, GLA_DV), jnp.float32)
    _, o_inter = lax.scan(step, state0, xs)
    o = o_intra + jnp.moveaxis(o_inter, 0, 1)
    return o.reshape(B, S, GLA_HEADS, GLA_DV).astype(dt)


def swa_sink_mixer(q, k, v, sinks):
    B, S = q.shape[0], q.shape[1]
    L = SWA_BLOCK
    n = S // L
    qb = q.reshape(B, n, L, SWA_KV_HEADS, SWA_GROUP, SWA_HEAD_DIM)
    pad = ((0, 0), (L, 0), (0, 0), (0, 0))
    kp = jnp.pad(k, pad).reshape(B, n + 1, L, SWA_KV_HEADS, SWA_HEAD_DIM)
    vp = jnp.pad(v, pad).reshape(B, n + 1, L, SWA_KV_HEADS, SWA_HEAD_DIM)
    kb = jnp.concatenate([kp[:, :-1], kp[:, 1:]], axis=2)
    vb = jnp.concatenate([vp[:, :-1], vp[:, 1:]], axis=2)
    s = jnp.einsum('bnqkgd,bnckd->bnkgqc', qb, kb).astype(jnp.float32) * (SWA_HEAD_DIM ** -0.5)
    qpos = jnp.arange(L)[:, None] + L
    kpos = jnp.arange(2 * L)[None, :]
    dist = (qpos - kpos)
    key_abs = (jnp.arange(n)[:, None] - 1) * L + jnp.arange(2 * L)[None, :]
    mask = ((dist >= 0) & (dist < SWA_WINDOW))[None] & (key_abs >= 0)[:, None, :]
    slopes = alibi_slopes(SWA_Q_HEADS).reshape(SWA_KV_HEADS, SWA_GROUP)
    s = s - slopes[:, :, None, None] * dist.astype(jnp.float32)
    s = jnp.where(mask[:, None, None], s, NEG_INF)
    sink = sinks.astype(jnp.float32).reshape(SWA_KV_HEADS, SWA_GROUP)[:, :, None, None]
    m = jnp.maximum(jnp.max(s, axis=-1, keepdims=True), sink)
    p = jnp.exp(s - m)
    probs = p / (jnp.sum(p, axis=-1, keepdims=True) + jnp.exp(sink - m))
    o = jnp.einsum('bnkgqc,bnckd->bnqkgd', probs.astype(v.dtype), vb)
    return o.reshape(B, S, SWA_Q_HEADS * SWA_HEAD_DIM)


def memory_attention(xq, mk, mv):
    s = jnp.einsum('bshd,bmhd->bhsm', xq, mk).astype(jnp.float32) * (X_HEAD_DIM ** -0.5)
    p = jax.nn.softmax(s, axis=-1)
    o = jnp.einsum('bhsm,bmhd->bshd', p.astype(mv.dtype), mv)
    return o.reshape(xq.shape[0], xq.shape[1], X_WIDTH)


def squared_relu_mlp(h, w_up, w_down):
    return jnp.square(jax.nn.relu(h @ w_up)) @ w_down


def setup_inputs(seed: int = 0) -> dict:
    key = jax.random.key(seed)
    ks = jax.random.split(key, 16)
    f32 = jnp.float32

    def w(k, shape, fan_in):
        return jax.random.normal(k, shape, f32) * (fan_in ** -0.5)

    def gain(k, shape):
        return 1.0 + 0.02 * jax.random.normal(k, shape, f32)

    return {
        'x': jax.random.normal(ks[0], (BATCH, SEQ, D_MODEL), f32),
        'mem': jax.random.normal(ks[1], (BATCH, MEM_LEN, D_MODEL), f32),
        'mem_norm_g': gain(ks[2], (D_MODEL,)),
        'attn_norm_g': gain(ks[3], (DEPTH, D_MODEL)),
        'w_in_gla': w(ks[4], (N_GLA_LAYERS, D_MODEL, GLA_IN), D_MODEL),
        'w_gate_up': w(ks[5], (N_GLA_LAYERS, GLA_GATE_RANK, GLA_K), GLA_GATE_RANK),
        'b_gate': 0.1 * jax.random.normal(ks[6], (N_GLA_LAYERS, GLA_K), f32),
        'gla_out_norm_g': gain(ks[7], (N_GLA_LAYERS, GLA_DV)),
        'w_in_swa': w(ks[8], (N_SWA_LAYERS, D_MODEL, SWA_IN), D_MODEL),
        'sinks': 0.5 * jax.random.normal(ks[9], (N_SWA_LAYERS, SWA_Q_HEADS), f32),
        'w_mem_kv': w(ks[10], (DEPTH, D_MODEL, 2 * X_WIDTH), D_MODEL),
        'w_out': w(ks[11], (DEPTH, MIX_WIDTH, D_MODEL), MIX_WIDTH),
        'mlp_norm_g': gain(ks[12], (DEPTH, D_MODEL)),
        'w_up': w(ks[13], (DEPTH, D_MODEL, D_FF), D_MODEL),
        'w_down': w(ks[14], (DEPTH, D_FF, D_MODEL), D_FF),
        'final_norm_g': gain(ks[15], (D_MODEL,)),
    }


def reference(x, mem, mem_norm_g, attn_norm_g, w_in_gla, w_gate_up, b_gate, gla_out_norm_g,
              w_in_swa, sinks, w_mem_kv, w_out, mlp_norm_g, w_up, w_down, final_norm_g):
    B, S = x.shape[0], x.shape[1]
    M = mem.shape[1]
    mem_n = rmsnorm(mem, mem_norm_g)
    gla_split = [GLA_K, 2 * GLA_K, 2 * GLA_K + GLA_V, 2 * GLA_K + 2 * GLA_V,
                 2 * GLA_K + 2 * GLA_V + GLA_GATE_RANK]
    q_w = SWA_Q_HEADS * SWA_HEAD_DIM
    kv_w = SWA_KV_HEADS * SWA_HEAD_DIM
    swa_split = [q_w, q_w + kv_w, q_w + 2 * kv_w]
    for i in range(DEPTH):
        h = rmsnorm(x, attn_norm_g[i])
        mkv = mem_n @ w_mem_kv[i]
        mk = mkv[..., :X_WIDTH].reshape(B, M, X_HEADS, X_HEAD_DIM)
        mv = mkv[..., X_WIDTH:].reshape(B, M, X_HEADS, X_HEAD_DIM)
        if i % N_MIXERS == 0:
            j = i // N_MIXERS
            proj = h @ w_in_gla[j]
            q, k, v, g_out, g_lr, xq = jnp.split(proj, gla_split, axis=-1)
            o = gla_mixer(q.reshape(B, S, GLA_HEADS, GLA_DK), k.reshape(B, S, GLA_HEADS, GLA_DK),
                          v.reshape(B, S, GLA_HEADS, GLA_DV), g_lr, w_gate_up[j], b_gate[j])
            o = rmsnorm(o, gla_out_norm_g[j]) * jax.nn.silu(g_out.reshape(B, S, GLA_HEADS, GLA_DV))
            o_mix = o.reshape(B, S, GLA_V)
        else:
            j = i // N_MIXERS
            proj = h @ w_in_swa[j]
            q, k, v, xq = jnp.split(proj, swa_split, axis=-1)
            o_mix = swa_sink_mixer(q.reshape(B, S, SWA_KV_HEADS, SWA_GROUP, SWA_HEAD_DIM),
                                   k.reshape(B, S, SWA_KV_HEADS, SWA_HEAD_DIM),
                                   v.reshape(B, S, SWA_KV_HEADS, SWA_HEAD_DIM), sinks[j])
        o_mem = memory_attention(xq.reshape(B, S, X_HEADS, X_HEAD_DIM), mk, mv)
        x = x + jnp.concatenate([o_mix, o_mem], axis=-1) @ w_out[i]
        x = x + squared_relu_mlp(rmsnorm(x, mlp_norm_g[i]), w_up[i], w_down[i])
    return rmsnorm(x, final_norm_g)
```

```python
import functools

import jax
import jax.numpy as jnp
from jax import lax
from jax.experimental import pallas as pl
from jax.experimental.pallas import tpu as pltpu

F32 = jnp.float32
BF16 = jnp.bfloat16

D_MODEL = 2048
DEPTH = 4
MEM_LEN = 256
RMS_EPS = 1e-5
NEG_INF = -1e30

X_HEADS = 4
X_HEAD_DIM = 128
X_WIDTH = X_HEADS * X_HEAD_DIM

GLA_HEADS = 4
GLA_DK = 256
GLA_DV = 384
GLA_K = GLA_HEADS * GLA_DK
GLA_V = GLA_HEADS * GLA_DV
GLA_GATE_RANK = 16
GLA_TAU = 16.0
GLA_CHUNK = 64
GLA_RANK_PAD = 128
GLA_PROJ = 2 * GLA_V + 2 * GLA_K + X_WIDTH + GLA_RANK_PAD

SWA_HEAD_DIM = 64
SWA_Q_HEADS = 24
SWA_GROUP = 8
SWA_KV_HEADS = 3
SWA_WINDOW = 128
SWA_BLOCK = 128
SWA_Q = SWA_Q_HEADS * SWA_HEAD_DIM
SWA_KV = SWA_KV_HEADS * SWA_HEAD_DIM
SWA_KV_PAD = 256
SWA_PROJ = SWA_Q + X_WIDTH + 2 * SWA_KV_PAD

D_FF = 4 * D_MODEL

MIB = 1024 * 1024


def _params(semantics, vmem_mib):
    return pltpu.CompilerParams(dimension_semantics=semantics,
                                vmem_limit_bytes=vmem_mib * MIB)


def _rmsnorm_rows(x_ref, g_ref, dst_ref, chunk=256):
    g = g_ref[...]

    def body(i, carry):
        r = pl.ds(pl.multiple_of(i * chunk, chunk), chunk)
        x = x_ref[r, :]
        ms = jnp.mean(x * x, axis=-1, keepdims=True)
        dst_ref[r, :] = (x * lax.rsqrt(ms + RMS_EPS) * g).astype(dst_ref.dtype)
        return carry

    lax.fori_loop(0, x_ref.shape[0] // chunk, body, 0)


def _norm_matmul_kernel(x_ref, g_ref, w_ref, o_ref, xn_ref):
    @pl.when(pl.program_id(1) == 0)
    def _():
        _rmsnorm_rows(x_ref, g_ref, xn_ref)

    o_ref[...] = jnp.dot(xn_ref[...], w_ref[...],
                         preferred_element_type=F32).astype(o_ref.dtype)


def _norm_matmul(x, g, w, tm, tn):
    t, d = x.shape
    if w.ndim == 3:
        per = w.shape[2] // tn
        n = w.shape[0] * w.shape[2]
        w_spec = pl.BlockSpec((None, d, tn), lambda i, j: (j // per, 0, j % per))
    else:
        n = w.shape[1]
        w_spec = pl.BlockSpec((d, tn), lambda i, j: (0, j))
    return pl.pallas_call(
        _norm_matmul_kernel,
        out_shape=jax.ShapeDtypeStruct((t, n), BF16),
        grid=(t // tm, n // tn),
        in_specs=[
            pl.BlockSpec((tm, d), lambda i, j: (i, 0)),
            pl.BlockSpec((1, d), lambda i, j: (0, 0)),
            w_spec,
        ],
        out_specs=pl.BlockSpec((tm, tn), lambda i, j: (i, j)),
        scratch_shapes=[pltpu.VMEM((tm, d), BF16)],
        compiler_params=_params(("parallel", "arbitrary"), 52),
        name="norm_matmul",
    )(x, g.reshape(1, d), w)


def _dot_t0(a, b):
    return lax.dot_general(a, b, (((0,), (0,)), ((), ())), preferred_element_type=F32)


def _dot_t1(a, b):
    return lax.dot_general(a, b, (((1,), (1,)), ((), ())), preferred_element_type=F32)


def _gla_kernel(q_ref, k_ref, v_ref, go_ref, glr_ref, wg_ref, bg_ref, gn_ref,
                o_ref, state_ref):
    @pl.when(pl.program_id(2) == 0)
    def _():
        state_ref[...] = jnp.zeros_like(state_ref)

    c = GLA_CHUNK
    row = lax.broadcasted_iota(jnp.int32, (c, c), 0)
    col = lax.broadcasted_iota(jnp.int32, (c, c), 1)
    causal = row >= col
    tri = jnp.where(causal, 1.0, 0.0).astype(BF16)
    ones = jnp.ones((c, 128), BF16)
    wg = wg_ref[...]
    bg = bg_ref[...]
    gn = gn_ref[...]

    def chunk(ci, carry):
        r = pl.ds(pl.multiple_of(ci * c, c), c)
        pre = jnp.dot(glr_ref[r, :], wg, preferred_element_type=F32) + bg
        log_sig = jnp.minimum(pre, 0.0) - jnp.log1p(jnp.exp(-jnp.abs(pre)))
        log_a = log_sig * (1.0 / GLA_TAU)
        hi = log_a.astype(BF16)
        rem = log_a - hi.astype(F32)
        mid = rem.astype(BF16)
        lo = (rem - mid.astype(F32)).astype(BF16)
        b = (jnp.dot(tri, hi, preferred_element_type=F32)
             + jnp.dot(tri, mid, preferred_element_type=F32)
             + jnp.dot(tri, lo, preferred_element_type=F32))
        b_last = b[c - 1:c, :]
        b_last_col = _dot_t0(hi, ones) + _dot_t0(mid, ones) + _dot_t0(lo, ones)

        q = q_ref[r, :].astype(F32) * (GLA_DK ** -0.5)
        k = k_ref[r, :].astype(F32)
        v = v_ref[r, :]
        q_in = (q * jnp.exp(b)).astype(BF16)
        k_in = (k * jnp.exp(-b)).astype(BF16)
        k_out = (k * jnp.exp(b_last - b)).astype(BF16)

        state = state_ref[...]
        a = jnp.where(causal, _dot_t1(q_in, k_in), 0.0).astype(BF16)
        o = (jnp.dot(a, v, preferred_element_type=F32)
             + jnp.dot(q_in, state.astype(BF16), preferred_element_type=F32))
        decay = jnp.exp(b_last_col)
        decay = jnp.concatenate([decay] * (GLA_DV // 128), axis=1)
        state_ref[...] = decay * state + _dot_t0(k_out, v)

        ms = jnp.mean(o * o, axis=-1, keepdims=True)
        y = o * lax.rsqrt(ms + RMS_EPS) * gn
        go = go_ref[r, :].astype(F32)
        o_ref[r, :] = (y * (go * jax.nn.sigmoid(go))).astype(o_ref.dtype)
        return carry

    lax.fori_loop(0, q_ref.shape[0] // c, chunk, 0)


def _gla_mixer(proj, w_gate, b_gate, out_g, batch, seq, ts):
    t = proj.shape[0]
    ns = seq // ts
    vb, gb = 0, GLA_V // GLA_DV
    qb, kb = 2 * GLA_V // GLA_DK, (2 * GLA_V + GLA_K) // GLA_DK
    rb = (2 * GLA_V + 2 * GLA_K + X_WIDTH) // GLA_RANK_PAD

    def rows(b, h, c):
        return b * ns + c

    return pl.pallas_call(
        _gla_kernel,
        out_shape=jax.ShapeDtypeStruct((t, GLA_V), BF16),
        grid=(batch, GLA_HEADS, ns),
        in_specs=[
            pl.BlockSpec((ts, GLA_DK), lambda b, h, c: (rows(b, h, c), qb + h)),
            pl.BlockSpec((ts, GLA_DK), lambda b, h, c: (rows(b, h, c), kb + h)),
            pl.BlockSpec((ts, GLA_DV), lambda b, h, c: (rows(b, h, c), vb + h)),
            pl.BlockSpec((ts, GLA_DV), lambda b, h, c: (rows(b, h, c), gb + h)),
            pl.BlockSpec((ts, GLA_RANK_PAD), lambda b, h, c: (rows(b, h, c), rb)),
            pl.BlockSpec((GLA_RANK_PAD, GLA_DK), lambda b, h, c: (0, h)),
            pl.BlockSpec((1, GLA_DK), lambda b, h, c: (0, h)),
            pl.BlockSpec((1, GLA_DV), lambda b, h, c: (0, 0)),
        ],
        out_specs=pl.BlockSpec((ts, GLA_DV), lambda b, h, c: (rows(b, h, c), h)),
        scratch_shapes=[pltpu.VMEM((GLA_DK, GLA_DV), F32)],
        compiler_params=_params(("parallel", "parallel", "arbitrary"), 32),
        name="gla_mixer",
    )(proj, proj, proj, proj, proj, w_gate, b_gate.reshape(1, GLA_K),
      out_g.reshape(1, GLA_DV))


def _swa_kernel(sinks_ref, q_ref, kvp_ref, kvc_ref, o_ref):
    n = pl.program_id(1)
    blk = SWA_BLOCK
    hd = SWA_HEAD_DIM
    qpos = lax.broadcasted_iota(jnp.int32, (blk, 2 * blk), 0) + blk
    kpos = lax.broadcasted_iota(jnp.int32, (blk, 2 * blk), 1)
    dist = qpos - kpos
    key_abs = (n - 1) * blk + kpos
    mask = (dist >= 0) & (dist < SWA_WINDOW) & (key_abs >= 0)
    distf = dist.astype(F32)
    for kv in range(SWA_KV_HEADS):
        ks = slice(kv * hd, (kv + 1) * hd)
        vs = slice(SWA_KV_PAD + kv * hd, SWA_KV_PAD + (kv + 1) * hd)
        kb = jnp.concatenate([kvp_ref[:, ks], kvc_ref[:, ks]], axis=0)
        vb = jnp.concatenate([kvp_ref[:, vs], kvc_ref[:, vs]], axis=0)
        for g in range(SWA_GROUP):
            h = kv * SWA_GROUP + g
            hs = slice(h * hd, (h + 1) * hd)
            slope = 2.0 ** (-8.0 * (h + 1) / SWA_Q_HEADS)
            s = _dot_t1(q_ref[:, hs], kb) * (hd ** -0.5)
            s = jnp.where(mask, s - slope * distf, NEG_INF)
            sink = sinks_ref[h]
            m = jnp.maximum(jnp.max(s, axis=-1, keepdims=True), sink)
            p = jnp.exp(s - m)
            denom = jnp.sum(p, axis=-1, keepdims=True) + jnp.exp(sink - m)
            oh = jnp.dot(p.astype(BF16), vb, preferred_element_type=F32) / denom
            o_ref[:, hs] = oh.astype(o_ref.dtype)


def _swa_mixer(proj, sinks, batch, seq):
    t = proj.shape[0]
    nb = seq // SWA_BLOCK
    kvb = (SWA_Q + X_WIDTH) // (2 * SWA_KV_PAD)
    return pl.pallas_call(
        _swa_kernel,
        out_shape=jax.ShapeDtypeStruct((t, SWA_Q), BF16),
        grid=(batch, nb),
        in_specs=[
            pl.BlockSpec(memory_space=pltpu.SMEM),
            pl.BlockSpec((SWA_BLOCK, SWA_Q), lambda b, n: (b * nb + n, 0)),
            pl.BlockSpec((SWA_BLOCK, 2 * SWA_KV_PAD),
                         lambda b, n: (b * nb + jnp.maximum(n - 1, 0), kvb)),
            pl.BlockSpec((SWA_BLOCK, 2 * SWA_KV_PAD), lambda b, n: (b * nb + n, kvb)),
        ],
        out_specs=pl.BlockSpec((SWA_BLOCK, SWA_Q), lambda b, n: (b * nb + n, 0)),
        compiler_params=_params(("parallel", "arbitrary"), 32),
        name="swa_mixer",
    )(sinks, proj, proj, proj)


def _memattn_kernel(xq_ref, mk_ref, mv_ref, o_ref):
    hd = X_HEAD_DIM
    for h in range(X_HEADS):
        hs = slice(h * hd, (h + 1) * hd)
        s = _dot_t1(xq_ref[:, hs], mk_ref[:, hs]) * (hd ** -0.5)
        m = jnp.max(s, axis=-1, keepdims=True)
        p = jnp.exp(s - m)
        denom = jnp.sum(p, axis=-1, keepdims=True)
        oh = jnp.dot(p.astype(BF16), mv_ref[:, hs], preferred_element_type=F32) / denom
        o_ref[:, hs] = oh.astype(o_ref.dtype)


def _mem_attention(proj, xq_block, mkv, layer, batch, seq, tm):
    t = proj.shape[0]
    ns = seq // tm
    return pl.pallas_call(
        _memattn_kernel,
        out_shape=jax.ShapeDtypeStruct((t, X_WIDTH), BF16),
        grid=(batch, ns),
        in_specs=[
            pl.BlockSpec((tm, X_WIDTH), lambda b, s: (b * ns + s, xq_block)),
            pl.BlockSpec((MEM_LEN, X_WIDTH), lambda b, s: (b, 2 * layer)),
            pl.BlockSpec((MEM_LEN, X_WIDTH), lambda b, s: (b, 2 * layer + 1)),
        ],
        out_specs=pl.BlockSpec((tm, X_WIDTH), lambda b, s: (b * ns + s, 0)),
        compiler_params=_params(("parallel", "arbitrary"), 32),
        name="mem_attention",
    )(proj, mkv, mkv)


def _out_proj_kernel(a_ref, b_ref, wa_ref, wb_ref, x_ref, o_ref):
    acc = jnp.dot(a_ref[...], wa_ref[...], preferred_element_type=F32)
    acc = acc + jnp.dot(b_ref[...], wb_ref[...], preferred_element_type=F32)
    o_ref[...] = x_ref[...] + acc


def _out_proj(o_mix, o_mem, wa, wb, x, tm):
    t, d = x.shape
    ka, kb = o_mix.shape[1], o_mem.shape[1]
    return pl.pallas_call(
        _out_proj_kernel,
        out_shape=jax.ShapeDtypeStruct((t, d), F32),
        grid=(t // tm,),
        in_specs=[
            pl.BlockSpec((tm, ka), lambda i: (i, 0)),
            pl.BlockSpec((tm, kb), lambda i: (i, 0)),
            pl.BlockSpec((ka, d), lambda i: (0, 0)),
            pl.BlockSpec((kb, d), lambda i: (0, 0)),
            pl.BlockSpec((tm, d), lambda i: (i, 0)),
        ],
        out_specs=pl.BlockSpec((tm, d), lambda i: (i, 0)),
        compiler_params=_params(("parallel",), 48),
        name="out_proj",
    )(o_mix, o_mem, wa, wb, x)


def _mlp_kernel(x_ref, g_ref, wu_ref, wd_ref, gf_ref, o_ref, xn_ref, *, final_norm):
    j = pl.program_id(1)

    @pl.when(j == 0)
    def _():
        _rmsnorm_rows(x_ref, g_ref, xn_ref)
        o_ref[...] = x_ref[...]

    h = jnp.dot(xn_ref[...], wu_ref[...], preferred_element_type=F32)
    h = jnp.square(jnp.maximum(h, 0.0)).astype(BF16)
    o_ref[...] += jnp.dot(h, wd_ref[...], preferred_element_type=F32)

    if final_norm:
        @pl.when(j == pl.num_programs(1) - 1)
        def _():
            _rmsnorm_rows(o_ref, gf_ref, o_ref)


def _mlp(x, g, w_up, w_down, g_final, final_norm, tm, tf):
    t, d = x.shape
    f = w_up.shape[1]
    return pl.pallas_call(
        functools.partial(_mlp_kernel, final_norm=final_norm),
        out_shape=jax.ShapeDtypeStruct((t, d), F32),
        grid=(t // tm, f // tf),
        in_specs=[
            pl.BlockSpec((tm, d), lambda i, j: (i, 0), pipeline_mode=pl.Buffered(1)),
            pl.BlockSpec((1, d), lambda i, j: (0, 0)),
            pl.BlockSpec((d, tf), lambda i, j: (0, j)),
            pl.BlockSpec((tf, d), lambda i, j: (j, 0)),
            pl.BlockSpec((1, d), lambda i, j: (0, 0)),
        ],
        out_specs=pl.BlockSpec((tm, d), lambda i, j: (i, 0)),
        scratch_shapes=[pltpu.VMEM((tm, d), BF16)],
        compiler_params=_params(("parallel", "arbitrary"), 52),
        name="mlp",
    )(x, g.reshape(1, d), w_up, w_down, g_final.reshape(1, d))


def _gla_in_weight(w):
    q = w[:, :GLA_K]
    k = w[:, GLA_K:2 * GLA_K]
    v = w[:, 2 * GLA_K:2 * GLA_K + GLA_V]
    go = w[:, 2 * GLA_K + GLA_V:2 * GLA_K + 2 * GLA_V]
    o = 2 * GLA_K + 2 * GLA_V
    glr = w[:, o:o + GLA_GATE_RANK]
    xq = w[:, o + GLA_GATE_RANK:]
    pad = jnp.zeros((w.shape[0], GLA_RANK_PAD - GLA_GATE_RANK), w.dtype)
    return jnp.concatenate([v, go, q, k, xq, glr, pad], axis=1).astype(BF16)


def _swa_in_weight(w):
    q = w[:, :SWA_Q]
    k = w[:, SWA_Q:SWA_Q + SWA_KV]
    v = w[:, SWA_Q + SWA_KV:SWA_Q + 2 * SWA_KV]
    xq = w[:, SWA_Q + 2 * SWA_KV:]
    pad = jnp.zeros((w.shape[0], SWA_KV_PAD - SWA_KV), w.dtype)
    return jnp.concatenate([q, xq, k, pad, v, pad], axis=1).astype(BF16)


def kernel(x, mem, mem_norm_g, attn_norm_g, w_in_gla, w_gate_up, b_gate, gla_out_norm_g,
           w_in_swa, sinks, w_mem_kv, w_out, mlp_norm_g, w_up, w_down, final_norm_g):
    batch, seq, d = x.shape
    t = batch * seq
    xf = x.reshape(t, d)

    mkv = _norm_matmul(mem.reshape(batch * MEM_LEN, d), mem_norm_g,
                       w_mem_kv.astype(BF16), tm=batch * MEM_LEN, tn=2 * X_WIDTH)

    for i in range(DEPTH):
        j = i // 2
        if i % 2 == 0:
            proj = _norm_matmul(xf, attn_norm_g[i], _gla_in_weight(w_in_gla[j]),
                                tm=1024, tn=GLA_PROJ // 5)
            w_gate = jnp.pad(w_gate_up[j], ((0, GLA_RANK_PAD - GLA_GATE_RANK), (0, 0))).astype(BF16)
            o_mix = _gla_mixer(proj, w_gate, b_gate[j], gla_out_norm_g[j], batch, seq, ts=512)
            xq_block = (2 * GLA_V + 2 * GLA_K) // X_WIDTH
        else:
            proj = _norm_matmul(xf, attn_norm_g[i], _swa_in_weight(w_in_swa[j]),
                                tm=1024, tn=SWA_PROJ // 2)
            o_mix = _swa_mixer(proj, sinks[j], batch, seq)
            xq_block = SWA_Q // X_WIDTH
        o_mem = _mem_attention(proj, xq_block, mkv, i, batch, seq, tm=512)
        wo = w_out[i].astype(BF16)
        kmix = o_mix.shape[1]
        xf = _out_proj(o_mix, o_mem, wo[:kmix], wo[kmix:], xf, tm=512)
        xf = _mlp(xf, mlp_norm_g[i], w_up[i].astype(BF16), w_down[i].astype(BF16),
                  final_norm_g, final_norm=(i == DEPTH - 1), tm=1024, tf=512)
    return xf.reshape(batch, seq, d)
```

```python
import functools

import jax
import jax.numpy as jnp
from jax import lax
from jax.experimental import pallas as pl
from jax.experimental.pallas import tpu as pltpu

F32 = jnp.float32
BF16 = jnp.bfloat16

D_MODEL = 2048
DEPTH = 4
MEM_LEN = 256
RMS_EPS = 1e-5
NEG_INF = -1e30

X_HEADS = 4
X_HEAD_DIM = 128
X_WIDTH = X_HEADS * X_HEAD_DIM

GLA_HEADS = 4
GLA_DK = 256
GLA_DV = 384
GLA_K = GLA_HEADS * GLA_DK
GLA_V = GLA_HEADS * GLA_DV
GLA_GATE_RANK = 16
GLA_TAU = 16.0
GLA_CHUNK = 64
GLA_SUB = 256
GLA_RANK_PAD = 128
GLA_PROJ = 2 * GLA_V + 2 * GLA_K + X_WIDTH + GLA_RANK_PAD

SWA_HEAD_DIM = 64
SWA_Q_HEADS = 24
SWA_GROUP = 8
SWA_KV_HEADS = 3
SWA_WINDOW = 128
SWA_BLOCK = 128
SWA_Q = SWA_Q_HEADS * SWA_HEAD_DIM
SWA_KV = SWA_KV_HEADS * SWA_HEAD_DIM
SWA_KV_PAD = 256
SWA_PROJ = SWA_Q + X_WIDTH + 2 * SWA_KV_PAD

D_FF = 4 * D_MODEL

LANES = 128
BF16_SUBLANES = 16
MIB = 1024 * 1024


def _params(semantics, vmem_mib):
    return pltpu.CompilerParams(dimension_semantics=semantics,
                                vmem_limit_bytes=vmem_mib * MIB)


def _rmsnorm_rows(x_ref, g_ref, dst_ref, chunk=256):
    g = g_ref[...]

    def body(i, carry):
        r = pl.ds(pl.multiple_of(i * chunk, chunk), chunk)
        x = x_ref[r, :]
        ms = jnp.mean(x * x, axis=-1, keepdims=True)
        dst_ref[r, :] = (x * lax.rsqrt(ms + RMS_EPS) * g).astype(dst_ref.dtype)
        return carry

    lax.fori_loop(0, x_ref.shape[0] // chunk, body, 0)


def _dot(a, b):
    return jnp.dot(a, b, preferred_element_type=F32)


def _dot_t0(a, b):
    return lax.dot_general(a, b, (((0,), (0,)), ((), ())), preferred_element_type=F32)


def _dot_t1(a, b):
    return lax.dot_general(a, b, (((1,), (1,)), ((), ())), preferred_element_type=F32)


def _split_bf16(x):
    hi = x.astype(BF16)
    lo = (x - hi.astype(F32)).astype(BF16)
    return hi, lo


def _norm_matmul_kernel(x_ref, g_ref, w_ref, o_ref, xn_ref):
    @pl.when(pl.program_id(1) == 0)
    def _():
        _rmsnorm_rows(x_ref, g_ref, xn_ref)

    o_ref[...] = _dot(xn_ref[...], w_ref[...]).astype(o_ref.dtype)


def _norm_matmul(x, g, w, tm, tn):
    t, d = x.shape
    if w.ndim == 3:
        per = w.shape[2] // tn
        n = w.shape[0] * w.shape[2]
        w_spec = pl.BlockSpec((None, d, tn), lambda i, j: (j // per, 0, j % per))
    else:
        n = w.shape[1]
        w_spec = pl.BlockSpec((d, tn), lambda i, j: (0, j))
    return pl.pallas_call(
        _norm_matmul_kernel,
        out_shape=jax.ShapeDtypeStruct((t, n), BF16),
        grid=(t // tm, n // tn),
        in_specs=[
            pl.BlockSpec((tm, d), lambda i, j: (i, 0)),
            pl.BlockSpec((1, d), lambda i, j: (0, 0)),
            w_spec,
        ],
        out_specs=pl.BlockSpec((tm, tn), lambda i, j: (i, j)),
        scratch_shapes=[pltpu.VMEM((tm, d), BF16)],
        compiler_params=_params(("parallel", "arbitrary"), 52),
        name="norm_matmul",
    )(x, g.reshape(1, d), w)


def _gla_kernel(q_ref, k_ref, v_ref, go_ref, glr_ref, wg_ref, bg_ref, gn_ref,
                o_ref, state_ref):
    @pl.when(pl.program_id(2) == 0)
    def _():
        state_ref[...] = jnp.zeros_like(state_ref)

    c, sub = GLA_CHUNK, GLA_SUB
    nc = sub // c
    shift = c.bit_length() - 1
    row = lax.broadcasted_iota(jnp.int32, (sub, sub), 0)
    col = lax.broadcasted_iota(jnp.int32, (sub, sub), 1)
    causal = (jnp.right_shift(row, shift) == jnp.right_shift(col, shift)) & (row >= col)
    tri = jnp.where(causal, 1.0, 0.0).astype(BF16)
    prow = lax.broadcasted_iota(jnp.int32, (BF16_SUBLANES, nc * LANES), 0)
    pcol = lax.broadcasted_iota(jnp.int32, (BF16_SUBLANES, nc * LANES), 1)
    sel = jnp.where((prow < 2 * nc) & ((prow & (nc - 1)) == pcol // LANES), 1.0, 0.0).astype(BF16)
    gn = gn_ref[...]
    heads = q_ref.shape[1] // GLA_DK

    def head_sub_block(hh, r):
        ks = slice(hh * GLA_DK, (hh + 1) * GLA_DK)
        vs = slice(hh * GLA_DV, (hh + 1) * GLA_DV)
        pre = _dot(glr_ref[r, :], wg_ref[:, ks]) + bg_ref[:, ks]
        yield
        log_sig = jnp.minimum(pre, 0.0) - jnp.log(1.0 + jnp.exp(-jnp.abs(pre)))
        log_a = log_sig * (1.0 / GLA_TAU)
        la_hi, la_lo = _split_bf16(log_a)
        b = _dot(tri, la_hi) + _dot(tri, la_lo)
        yield
        b_last = [b[(i + 1) * c - 1:(i + 1) * c, :] for i in range(nc)]
        bl_hi, bl_lo = _split_bf16(jnp.concatenate(b_last, axis=0))
        pad = jnp.zeros((BF16_SUBLANES - 2 * nc, GLA_DK), BF16)
        b_last_cols = _dot_t0(jnp.concatenate([bl_hi, bl_lo, pad], axis=0), sel)
        b_last_rows = jnp.concatenate(
            [jnp.broadcast_to(x, (c, GLA_DK)) for x in b_last], axis=0)

        q = q_ref[r, ks].astype(F32) * (GLA_DK ** -0.5)
        k = k_ref[r, ks].astype(F32)
        v = v_ref[r, vs]
        q_in = (q * jnp.exp(b)).astype(BF16)
        k_in = (k * jnp.exp(-b)).astype(BF16)
        a_raw = _dot_t1(q_in, k_in)
        k_out = (k * jnp.exp(b_last_rows - b)).astype(BF16)
        kv = [_dot_t0(k_out[i * c:(i + 1) * c], v[i * c:(i + 1) * c]) for i in range(nc)]
        yield
        a = jnp.where(causal, a_raw, 0.0).astype(BF16)
        o_intra = _dot(a, v)
        decays = []
        for i in range(nc):
            decay = jnp.exp(b_last_cols[:, i * LANES:(i + 1) * LANES])
            decays.append(jnp.concatenate([decay] * (GLA_DV // LANES), axis=1))
        yield
        state = state_ref[hh]
        outs = []
        for i in range(nc):
            cs = slice(i * c, (i + 1) * c)
            outs.append(o_intra[cs] + _dot(q_in[cs], state.astype(BF16)))
            state = decays[i] * state + kv[i]
        state_ref[hh] = state
        yield
        o = jnp.concatenate(outs, axis=0)
        ms = jnp.mean(o * o, axis=-1, keepdims=True)
        y = o * lax.rsqrt(ms + RMS_EPS) * gn
        go = go_ref[r, vs].astype(F32)
        o_ref[r, vs] = (y * (go * jax.nn.sigmoid(go))).astype(o_ref.dtype)

    def sub_block(si, carry):
        r = pl.ds(pl.multiple_of(si * sub, sub), sub)
        pending = [head_sub_block(hh, r) for hh in range(heads)]
        while pending:
            pending = [gen for gen in pending if next(gen, True) is None]
        return carry

    lax.fori_loop(0, q_ref.shape[0] // sub, sub_block, 0)


def _gla_mixer(proj, w_gate, b_gate, out_g, batch, seq, ts, hpb):
    t = proj.shape[0]
    ns = seq // ts
    kw, vw = hpb * GLA_DK, hpb * GLA_DV
    vb, gb = 0, GLA_V // vw
    qb, kb = 2 * GLA_V // kw, (2 * GLA_V + GLA_K) // kw
    rb = (2 * GLA_V + 2 * GLA_K + X_WIDTH) // GLA_RANK_PAD

    def rows(b, h, c):
        return b * ns + c

    return pl.pallas_call(
        _gla_kernel,
        out_shape=jax.ShapeDtypeStruct((t, GLA_V), BF16),
        grid=(batch, GLA_HEADS // hpb, ns),
        in_specs=[
            pl.BlockSpec((ts, kw), lambda b, h, c: (rows(b, h, c), qb + h)),
            pl.BlockSpec((ts, kw), lambda b, h, c: (rows(b, h, c), kb + h)),
            pl.BlockSpec((ts, vw), lambda b, h, c: (rows(b, h, c), vb + h)),
            pl.BlockSpec((ts, vw), lambda b, h, c: (rows(b, h, c), gb + h)),
            pl.BlockSpec((ts, GLA_RANK_PAD), lambda b, h, c: (rows(b, h, c), rb)),
            pl.BlockSpec((GLA_RANK_PAD, kw), lambda b, h, c: (0, h)),
            pl.BlockSpec((1, kw), lambda b, h, c: (0, h)),
            pl.BlockSpec((1, GLA_DV), lambda b, h, c: (0, 0)),
        ],
        out_specs=pl.BlockSpec((ts, vw), lambda b, h, c: (rows(b, h, c), h)),
        scratch_shapes=[pltpu.VMEM((hpb, GLA_DK, GLA_DV), F32)],
        compiler_params=_params(("parallel", "parallel", "arbitrary"), 40),
        name="gla_mixer",
    )(proj, proj, proj, proj, proj, w_gate, b_gate.reshape(1, GLA_K),
      out_g.reshape(1, GLA_DV))


def _swa_kernel(sinks_ref, q_ref, kvp_ref, kvc_ref, o_ref, bias_ref):
    n = pl.program_id(1)
    blk = SWA_BLOCK
    hd = SWA_HEAD_DIM

    @pl.when(n <= 1)
    def _():
        qpos = lax.broadcasted_iota(jnp.int32, (blk, 2 * blk), 0) + blk
        kpos = lax.broadcasted_iota(jnp.int32, (blk, 2 * blk), 1)
        dist = qpos - kpos
        key_abs = (n - 1) * blk + kpos
        mask = (dist >= 0) & (dist < SWA_WINDOW) & (key_abs >= 0)
        distf = dist.astype(F32)
        for h in range(SWA_Q_HEADS):
            slope = 2.0 ** (-8.0 * (h + 1) / SWA_Q_HEADS)
            bias_ref[h] = jnp.where(mask, -slope * distf, NEG_INF)

    ones = jnp.ones((2 * blk, hd), BF16)
    low_half = lax.broadcasted_iota(jnp.int32, (blk, 2 * hd), 1) < hd
    half = SWA_GROUP // 2

    def kv_group(kv):
        ks = slice(kv * hd, (kv + 1) * hd)
        vs = slice(SWA_KV_PAD + kv * hd, SWA_KV_PAD + (kv + 1) * hd)
        kb = jnp.concatenate([kvp_ref[:, ks], kvc_ref[:, ks]], axis=0)
        vb = jnp.concatenate([kvp_ref[:, vs], kvc_ref[:, vs]], axis=0)
        v_even = jnp.concatenate([vb, ones], axis=1)
        v_odd = jnp.concatenate([ones, vb], axis=1)
        first = kv * SWA_GROUP
        heads = list(range(first, first + SWA_GROUP, 2)) + list(range(first + 1, first + SWA_GROUP, 2))
        qs = jnp.concatenate([q_ref[:, h * hd:(h + 1) * hd] for h in heads], axis=0)
        s = _dot_t1(qs * (hd ** -0.5), kb)
        yield

        def softmax_half(lo):
            ps, es = [], []
            for g in range(lo, lo + half):
                h = heads[g]
                sg = s[g * blk:(g + 1) * blk] + bias_ref[h]
                m = jnp.maximum(jnp.max(sg, axis=-1, keepdims=True), sinks_ref[h])
                ps.append(jnp.exp(sg - m).astype(BF16))
                es.append(jnp.exp(sinks_ref[h] - m))
            return jnp.concatenate(ps, axis=0), es

        p_even, e_even = softmax_half(0)
        pv_even = _dot(p_even, v_even)
        yield
        p_odd, e_odd = softmax_half(half)
        pv_odd = _dot(p_odd, v_odd)
        yield
        for t in range(half):
            rows = slice(t * blk, (t + 1) * blk)
            pe, po = pv_even[rows], pv_odd[rows]
            num = jnp.where(low_half, pe, po)
            den = pltpu.roll(jnp.where(low_half, po, pe), hd, axis=1)
            den = den + jnp.where(low_half, e_even[t], e_odd[t])
            pair = slice((first + 2 * t) * hd, (first + 2 * t + 2) * hd)
            o_ref[:, pair] = (num / den).astype(o_ref.dtype)

    groups = [kv_group(kv) for kv in range(SWA_KV_HEADS)]
    for kv in (0, 1, 0, 0, 2, 1, 0, 1, 2, 1, 2, 2):
        next(groups[kv], None)


def _swa_mixer(proj, sinks, batch, seq):
    t = proj.shape[0]
    nb = seq // SWA_BLOCK
    kvb = (SWA_Q + X_WIDTH) // (2 * SWA_KV_PAD)
    return pl.pallas_call(
        _swa_kernel,
        out_shape=jax.ShapeDtypeStruct((t, SWA_Q), BF16),
        grid=(batch, nb),
        in_specs=[
            pl.BlockSpec(memory_space=pltpu.SMEM),
            pl.BlockSpec((SWA_BLOCK, SWA_Q), lambda b, n: (b * nb + n, 0)),
            pl.BlockSpec((SWA_BLOCK, 2 * SWA_KV_PAD),
                         lambda b, n: (b * nb + jnp.maximum(n - 1, 0), kvb)),
            pl.BlockSpec((SWA_BLOCK, 2 * SWA_KV_PAD), lambda b, n: (b * nb + n, kvb)),
        ],
        out_specs=pl.BlockSpec((SWA_BLOCK, SWA_Q), lambda b, n: (b * nb + n, 0)),
        scratch_shapes=[pltpu.VMEM((SWA_Q_HEADS, SWA_BLOCK, 2 * SWA_BLOCK), F32)],
        compiler_params=_params(("parallel", "arbitrary"), 32),
        name="swa_mixer",
    )(sinks, proj, proj, proj)


def _memattn_kernel(xq_ref, mk_ref, mv_ref, o_ref):
    hd = X_HEAD_DIM
    for h in range(X_HEADS):
        hs = slice(h * hd, (h + 1) * hd)
        s = _dot_t1(xq_ref[:, hs], mk_ref[:, hs]) * (hd ** -0.5)
        m = jnp.max(s, axis=-1, keepdims=True)
        p = jnp.exp(s - m)
        denom = jnp.sum(p, axis=-1, keepdims=True)
        oh = _dot(p.astype(BF16), mv_ref[:, hs]) / denom
        o_ref[:, hs] = oh.astype(o_ref.dtype)


def _mem_attention(proj, xq_block, mkv, layer, batch, seq, tm):
    t = proj.shape[0]
    ns = seq // tm
    return pl.pallas_call(
        _memattn_kernel,
        out_shape=jax.ShapeDtypeStruct((t, X_WIDTH), BF16),
        grid=(batch, ns),
        in_specs=[
            pl.BlockSpec((tm, X_WIDTH), lambda b, s: (b * ns + s, xq_block)),
            pl.BlockSpec((MEM_LEN, X_WIDTH), lambda b, s: (b, 2 * layer)),
            pl.BlockSpec((MEM_LEN, X_WIDTH), lambda b, s: (b, 2 * layer + 1)),
        ],
        out_specs=pl.BlockSpec((tm, X_WIDTH), lambda b, s: (b * ns + s, 0)),
        compiler_params=_params(("parallel", "arbitrary"), 32),
        name="mem_attention",
    )(proj, mkv, mkv)


def _out_proj_kernel(a_ref, b_ref, wa_ref, wb_ref, x_ref, o_ref):
    o_ref[...] = x_ref[...] + (_dot(a_ref[...], wa_ref[...]) + _dot(b_ref[...], wb_ref[...]))


def _out_proj(o_mix, o_mem, w_out, layer, x, tm):
    t, d = x.shape
    ka, kb = o_mix.shape[1], o_mem.shape[1]
    return pl.pallas_call(
        _out_proj_kernel,
        out_shape=jax.ShapeDtypeStruct((t, d), F32),
        grid=(t // tm,),
        in_specs=[
            pl.BlockSpec((tm, ka), lambda i: (i, 0)),
            pl.BlockSpec((tm, kb), lambda i: (i, 0)),
            pl.BlockSpec((None, ka, d), lambda i: (layer, 0, 0)),
            pl.BlockSpec((None, kb, d), lambda i: (layer, ka // kb, 0)),
            pl.BlockSpec((tm, d), lambda i: (i, 0)),
        ],
        out_specs=pl.BlockSpec((tm, d), lambda i: (i, 0)),
        compiler_params=_params(("parallel",), 48),
        name="out_proj",
    )(o_mix, o_mem, w_out, w_out, x)


def _mlp_kernel(x_ref, g_ref, wu_ref, wd_ref, gf_ref, o_ref, xn_ref, *, final_norm):
    j = pl.program_id(1)

    @pl.when(j == 0)
    def _():
        _rmsnorm_rows(x_ref, g_ref, xn_ref)
        o_ref[...] = x_ref[...]

    h = _dot(xn_ref[...], wu_ref[...].astype(BF16))
    h = jnp.square(jnp.maximum(h, 0.0)).astype(BF16)
    o_ref[...] += _dot(h, wd_ref[...].astype(BF16))

    if final_norm:
        @pl.when(j == pl.num_programs(1) - 1)
        def _():
            _rmsnorm_rows(o_ref, gf_ref, o_ref)


def _mlp(x, g, w_up, w_down, layer, g_final, final_norm, tm, tf):
    t, d = x.shape
    f = w_up.shape[2]
    return pl.pallas_call(
        functools.partial(_mlp_kernel, final_norm=final_norm),
        out_shape=jax.ShapeDtypeStruct((t, d), F32),
        grid=(t // tm, f // tf),
        in_specs=[
            pl.BlockSpec((tm, d), lambda i, j: (i, 0), pipeline_mode=pl.Buffered(1)),
            pl.BlockSpec((1, d), lambda i, j: (0, 0)),
            pl.BlockSpec((None, d, tf), lambda i, j: (layer, 0, j)),
            pl.BlockSpec((None, tf, d), lambda i, j: (layer, j, 0)),
            pl.BlockSpec((1, d), lambda i, j: (0, 0)),
        ],
        out_specs=pl.BlockSpec((tm, d), lambda i, j: (i, 0)),
        scratch_shapes=[pltpu.VMEM((tm, d), BF16)],
        compiler_params=_params(("parallel", "arbitrary"), 58),
        name="mlp",
    )(x, g.reshape(1, d), w_up, w_down, g_final.reshape(1, d))


def _gla_in_weight(w):
    q = w[:, :GLA_K]
    k = w[:, GLA_K:2 * GLA_K]
    v = w[:, 2 * GLA_K:2 * GLA_K + GLA_V]
    go = w[:, 2 * GLA_K + GLA_V:2 * GLA_K + 2 * GLA_V]
    o = 2 * GLA_K + 2 * GLA_V
    glr = w[:, o:o + GLA_GATE_RANK]
    xq = w[:, o + GLA_GATE_RANK:]
    pad = jnp.zeros((w.shape[0], GLA_RANK_PAD - GLA_GATE_RANK), w.dtype)
    return jnp.concatenate([v, go, q, k, xq, glr, pad], axis=1).astype(BF16)


def _swa_in_weight(w):
    q = w[:, :SWA_Q]
    k = w[:, SWA_Q:SWA_Q + SWA_KV]
    v = w[:, SWA_Q + SWA_KV:SWA_Q + 2 * SWA_KV]
    xq = w[:, SWA_Q + 2 * SWA_KV:]
    pad = jnp.zeros((w.shape[0], SWA_KV_PAD - SWA_KV), w.dtype)
    return jnp.concatenate([q, xq, k, pad, v, pad], axis=1).astype(BF16)


def kernel(x, mem, mem_norm_g, attn_norm_g, w_in_gla, w_gate_up, b_gate, gla_out_norm_g,
           w_in_swa, sinks, w_mem_kv, w_out, mlp_norm_g, w_up, w_down, final_norm_g):
    batch, seq, d = x.shape
    t = batch * seq
    xf = x.reshape(t, d)
    w_out_bf = w_out.astype(BF16)

    mkv = _norm_matmul(mem.reshape(batch * MEM_LEN, d), mem_norm_g,
                       w_mem_kv.astype(BF16), tm=batch * MEM_LEN, tn=2 * X_WIDTH)

    for i in range(DEPTH):
        j = i // 2
        if i % 2 == 0:
            proj = _norm_matmul(xf, attn_norm_g[i], _gla_in_weight(w_in_gla[j]),
                                tm=1024, tn=GLA_PROJ // 3)
            w_gate = jnp.pad(w_gate_up[j], ((0, GLA_RANK_PAD - GLA_GATE_RANK), (0, 0))).astype(BF16)
            o_mix = _gla_mixer(proj, w_gate, b_gate[j], gla_out_norm_g[j], batch, seq,
                               ts=1024, hpb=4)
            xq_block = (2 * GLA_V + 2 * GLA_K) // X_WIDTH
        else:
            proj = _norm_matmul(xf, attn_norm_g[i], _swa_in_weight(w_in_swa[j]),
                                tm=1024, tn=SWA_PROJ // 2)
            o_mix = _swa_mixer(proj, sinks[j], batch, seq)
            xq_block = SWA_Q // X_WIDTH
        o_mem = _mem_attention(proj, xq_block, mkv, i, batch, seq, tm=512)
        xf = _out_proj(o_mix, o_mem, w_out_bf, i, xf, tm=512)
        xf = _mlp(xf, mlp_norm_g[i], w_up, w_down, i, final_norm_g,
                  final_norm=(i == DEPTH - 1), tm=1024, tf=512)
    return xf.reshape(batch, seq, d)
```

```python
import functools

import jax
import jax.numpy as jnp
from jax import lax
from jax.experimental import pallas as pl
from jax.experimental.pallas import tpu as pltpu

F32 = jnp.float32
BF16 = jnp.bfloat16

D_MODEL = 2048
DEPTH = 4
MEM_LEN = 256
RMS_EPS = 1e-5
NEG_INF = -1e30

X_HEADS = 4
X_HEAD_DIM = 128
X_WIDTH = X_HEADS * X_HEAD_DIM

GLA_HEADS = 4
GLA_DK = 256
GLA_DV = 384
GLA_K = GLA_HEADS * GLA_DK
GLA_V = GLA_HEADS * GLA_DV
GLA_GATE_RANK = 16
GLA_TAU = 16.0
GLA_CHUNK = 64
GLA_SUB = 256
GLA_RANK_PAD = 128
IN_TILE = 512
GLA_PROJ = 2 * GLA_V + 2 * GLA_K + X_WIDTH + GLA_RANK_PAD
GLA_PIECES = (
    tuple((2 * GLA_K + i * IN_TILE, i * IN_TILE) for i in range(2 * GLA_V // IN_TILE))
    + tuple((i * IN_TILE, 2 * GLA_V + i * IN_TILE) for i in range(2 * GLA_K // IN_TILE)))

SWA_HEAD_DIM = 64
SWA_Q_HEADS = 24
SWA_GROUP = 8
SWA_KV_HEADS = 3
SWA_WINDOW = 128
SWA_BLOCK = 128
SWA_Q = SWA_Q_HEADS * SWA_HEAD_DIM
SWA_KV = SWA_KV_HEADS * SWA_HEAD_DIM
SWA_PROJ = SWA_Q + X_WIDTH + IN_TILE
SWA_PIECES = (
    tuple((i * IN_TILE, i * IN_TILE) for i in range(SWA_Q // IN_TILE))
    + ((SWA_Q + 2 * SWA_KV, SWA_Q), (SWA_Q, SWA_Q + X_WIDTH)))

D_FF = 4 * D_MODEL

LANES = 128
BF16_SUBLANES = 16
MIB = 1024 * 1024


def _params(semantics, vmem_mib):
    return pltpu.CompilerParams(dimension_semantics=semantics,
                                vmem_limit_bytes=vmem_mib * MIB)


def _rmsnorm_rows(x_ref, g_ref, dst_ref, chunk=256):
    g = g_ref[...]

    def body(i, carry):
        r = pl.ds(pl.multiple_of(i * chunk, chunk), chunk)
        x = x_ref[r, :]
        ms = jnp.mean(x * x, axis=-1, keepdims=True)
        dst_ref[r, :] = (x * lax.rsqrt(ms + RMS_EPS) * g).astype(dst_ref.dtype)
        return carry

    lax.fori_loop(0, x_ref.shape[0] // chunk, body, 0)


def _dot(a, b):
    return jnp.dot(a, b, preferred_element_type=F32)


def _dot_t0(a, b):
    return lax.dot_general(a, b, (((0,), (0,)), ((), ())), preferred_element_type=F32)


def _dot_t1(a, b):
    return lax.dot_general(a, b, (((1,), (1,)), ((), ())), preferred_element_type=F32)


def _split_bf16(x):
    hi = x.astype(BF16)
    lo = (x - hi.astype(F32)).astype(BF16)
    return hi, lo


def _row_tile_fetch(x_hbm, xbuf, sem):
    i = pl.program_id(0)
    tm = xbuf.shape[0]

    def copy(idx):
        rows = pl.ds(pl.multiple_of(idx * tm, tm), tm)
        return pltpu.make_async_copy(x_hbm.at[rows, :], xbuf, sem)

    @pl.when(i == 0)
    def _():
        copy(0).start()

    copy(i).wait()

    def request_next():
        @pl.when(i + 1 < pl.num_programs(0))
        def _():
            copy(i + 1).start()

    return request_next


def _norm_matmul_kernel(x_ref, g_ref, w_ref, o_ref, xn_ref):
    @pl.when(pl.program_id(1) == 0)
    def _():
        _rmsnorm_rows(x_ref, g_ref, xn_ref)

    o_ref[...] = _dot(xn_ref[...], w_ref[...].astype(BF16)).astype(o_ref.dtype)


def _norm_matmul(x, g, w, tm):
    t, d = x.shape
    layers, _, n1 = w.shape
    return pl.pallas_call(
        _norm_matmul_kernel,
        out_shape=jax.ShapeDtypeStruct((t, layers * n1), BF16),
        grid=(t // tm, layers),
        in_specs=[
            pl.BlockSpec((tm, d), lambda i, j: (i, 0)),
            pl.BlockSpec((1, d), lambda i, j: (0, 0)),
            pl.BlockSpec((None, d, n1), lambda i, j: (j, 0, 0)),
        ],
        out_specs=pl.BlockSpec((tm, n1), lambda i, j: (i, j)),
        scratch_shapes=[pltpu.VMEM((tm, d), BF16)],
        compiler_params=_params(("parallel", "arbitrary"), 52),
        name="norm_matmul",
    )(x, g.reshape(1, d), w)


def _in_proj_kernel(x_hbm, g_ref, w_hbm, *rest, layer, pieces, tail_dst):
    if tail_dst is None:
        wt_ref = None
        o_ref, xbuf, xn_ref, wbuf, sems = rest
    else:
        wt_ref, o_ref, xbuf, xn_ref, wbuf, sems = rest
    tn = wbuf.shape[2]

    def tile_copy(p, slot):
        src = pieces[p][0]
        return pltpu.make_async_copy(w_hbm.at[layer, :, pl.ds(src, tn)], wbuf.at[slot],
                                     sems.at[1 + slot])

    tile_copy(0, 0).start()
    request_next_rows = _row_tile_fetch(x_hbm, xbuf, sems.at[0])
    _rmsnorm_rows(xbuf, g_ref, xn_ref)
    request_next_rows()
    if wt_ref is not None:
        o_ref[:, tail_dst:] = _dot(xn_ref[...], wt_ref[...]).astype(o_ref.dtype)
    for p, (_, dst) in enumerate(pieces):
        slot = p % 2
        if p + 1 < len(pieces):
            tile_copy(p + 1, 1 - slot).start()
        tile_copy(p, slot).wait()
        o_ref[:, dst:dst + tn] = _dot(xn_ref[...], wbuf[slot].astype(BF16)).astype(o_ref.dtype)


def _in_proj(x, g, w, layer, pieces, n_out, w_tail, tm):
    t, d = x.shape
    tail_dst = None if w_tail is None else n_out - w_tail.shape[1]
    in_specs = [
        pl.BlockSpec(memory_space=pl.ANY),
        pl.BlockSpec((1, d), lambda i: (0, 0)),
        pl.BlockSpec(memory_space=pl.ANY),
    ]
    args = [x, g.reshape(1, d), w]
    if w_tail is not None:
        in_specs.append(pl.BlockSpec(w_tail.shape, lambda i: (0, 0)))
        args.append(w_tail)
    return pl.pallas_call(
        functools.partial(_in_proj_kernel, layer=layer, pieces=pieces, tail_dst=tail_dst),
        out_shape=jax.ShapeDtypeStruct((t, n_out), BF16),
        grid=(t // tm,),
        in_specs=in_specs,
        out_specs=pl.BlockSpec((tm, n_out), lambda i: (i, 0)),
        scratch_shapes=[
            pltpu.VMEM((tm, d), F32),
            pltpu.VMEM((tm, d), BF16),
            pltpu.VMEM((2, d, IN_TILE), F32),
            pltpu.SemaphoreType.DMA((3,)),
        ],
        compiler_params=_params(("arbitrary",), 58),
        name="in_proj",
    )(*args)


def _gla_kernel(q_ref, k_ref, v_ref, go_ref, glr_ref, wg_ref, bg_ref, gn_ref,
                o_ref, state_ref):
    @pl.when(pl.program_id(2) == 0)
    def _():
        state_ref[...] = jnp.zeros_like(state_ref)

    c, sub = GLA_CHUNK, GLA_SUB
    nc = sub // c
    shift = c.bit_length() - 1
    row = lax.broadcasted_iota(jnp.int32, (sub, sub), 0)
    col = lax.broadcasted_iota(jnp.int32, (sub, sub), 1)
    causal = (jnp.right_shift(row, shift) == jnp.right_shift(col, shift)) & (row >= col)
    tri = jnp.where(causal, 1.0, 0.0).astype(BF16)
    prow = lax.broadcasted_iota(jnp.int32, (BF16_SUBLANES, nc * LANES), 0)
    pcol = lax.broadcasted_iota(jnp.int32, (BF16_SUBLANES, nc * LANES), 1)
    sel = jnp.where((prow < 2 * nc) & ((prow & (nc - 1)) == pcol // LANES), 1.0, 0.0).astype(BF16)
    gn = gn_ref[...]
    heads = q_ref.shape[1] // GLA_DK

    def head_sub_block(hh, r):
        ks = slice(hh * GLA_DK, (hh + 1) * GLA_DK)
        vs = slice(hh * GLA_DV, (hh + 1) * GLA_DV)
        pre = _dot(glr_ref[r, :], wg_ref[:, ks]) + bg_ref[:, ks]
        yield
        log_sig = jnp.minimum(pre, 0.0) - jnp.log(1.0 + jnp.exp(-jnp.abs(pre)))
        log_a = log_sig * (1.0 / GLA_TAU)
        la_hi, la_lo = _split_bf16(log_a)
        b = _dot(tri, la_hi) + _dot(tri, la_lo)
        yield
        b_last = [b[(i + 1) * c - 1:(i + 1) * c, :] for i in range(nc)]
        bl_hi, bl_lo = _split_bf16(jnp.concatenate(b_last, axis=0))
        pad = jnp.zeros((BF16_SUBLANES - 2 * nc, GLA_DK), BF16)
        b_last_cols = _dot_t0(jnp.concatenate([bl_hi, bl_lo, pad], axis=0), sel)
        b_last_rows = jnp.concatenate(
            [jnp.broadcast_to(x, (c, GLA_DK)) for x in b_last], axis=0)

        q = q_ref[r, ks].astype(F32) * (GLA_DK ** -0.5)
        k = k_ref[r, ks].astype(F32)
        v = v_ref[r, vs]
        q_in = (q * jnp.exp(b)).astype(BF16)
        k_in = (k * jnp.exp(-b)).astype(BF16)
        a_raw = _dot_t1(q_in, k_in)
        k_out = (k * jnp.exp(b_last_rows - b)).astype(BF16)
        kv = [_dot_t0(k_out[i * c:(i + 1) * c], v[i * c:(i + 1) * c]) for i in range(nc)]
        yield
        a = jnp.where(causal, a_raw, 0.0).astype(BF16)
        o_intra = _dot(a, v)
        decays = []
        for i in range(nc):
            decay = jnp.exp(b_last_cols[:, i * LANES:(i + 1) * LANES])
            decays.append(jnp.concatenate([decay] * (GLA_DV // LANES), axis=1))
        yield
        state = state_ref[hh]
        outs = []
        for i in range(nc):
            cs = slice(i * c, (i + 1) * c)
            outs.append(o_intra[cs] + _dot(q_in[cs], state.astype(BF16)))
            state = decays[i] * state + kv[i]
        state_ref[hh] = state
        yield
        o = jnp.concatenate(outs, axis=0)
        ms = jnp.mean(o * o, axis=-1, keepdims=True)
        y = o * lax.rsqrt(ms + RMS_EPS) * gn
        go = go_ref[r, vs].astype(F32)
        o_ref[r, vs] = (y * (go * jax.nn.sigmoid(go))).astype(o_ref.dtype)

    def sub_block(si, carry):
        r = pl.ds(pl.multiple_of(si * sub, sub), sub)
        pending = [head_sub_block(hh, r) for hh in range(heads)]
        while pending:
            pending = [gen for gen in pending if next(gen, True) is None]
        return carry

    lax.fori_loop(0, q_ref.shape[0] // sub, sub_block, 0)


def _gla_mixer(proj, w_gate, b_gate, out_g, batch, seq, ts, hpb):
    t = proj.shape[0]
    ns = seq // ts
    kw, vw = hpb * GLA_DK, hpb * GLA_DV
    vb, gb = 0, GLA_V // vw
    qb, kb = 2 * GLA_V // kw, (2 * GLA_V + GLA_K) // kw
    rb = (2 * GLA_V + 2 * GLA_K + X_WIDTH) // GLA_RANK_PAD

    def rows(b, h, c):
        return b * ns + c

    return pl.pallas_call(
        _gla_kernel,
        out_shape=jax.ShapeDtypeStruct((t, GLA_V), BF16),
        grid=(batch, GLA_HEADS // hpb, ns),
        in_specs=[
            pl.BlockSpec((ts, kw), lambda b, h, c: (rows(b, h, c), qb + h)),
            pl.BlockSpec((ts, kw), lambda b, h, c: (rows(b, h, c), kb + h)),
            pl.BlockSpec((ts, vw), lambda b, h, c: (rows(b, h, c), vb + h)),
            pl.BlockSpec((ts, vw), lambda b, h, c: (rows(b, h, c), gb + h)),
            pl.BlockSpec((ts, GLA_RANK_PAD), lambda b, h, c: (rows(b, h, c), rb)),
            pl.BlockSpec((GLA_RANK_PAD, kw), lambda b, h, c: (0, h)),
            pl.BlockSpec((1, kw), lambda b, h, c: (0, h)),
            pl.BlockSpec((1, GLA_DV), lambda b, h, c: (0, 0)),
        ],
        out_specs=pl.BlockSpec((ts, vw), lambda b, h, c: (rows(b, h, c), h)),
        scratch_shapes=[pltpu.VMEM((hpb, GLA_DK, GLA_DV), F32)],
        compiler_params=_params(("parallel", "parallel", "arbitrary"), 40),
        name="gla_mixer",
    )(proj, proj, proj, proj, proj, w_gate, b_gate.reshape(1, GLA_K),
      out_g.reshape(1, GLA_DV))


def _swa_kernel(sinks_ref, q_ref, kvp_ref, kvc_ref, o_ref, bias_ref):
    n = pl.program_id(1)
    blk = SWA_BLOCK
    hd = SWA_HEAD_DIM

    @pl.when(n <= 1)
    def _():
        qpos = lax.broadcasted_iota(jnp.int32, (blk, 2 * blk), 0) + blk
        kpos = lax.broadcasted_iota(jnp.int32, (blk, 2 * blk), 1)
        dist = qpos - kpos
        key_abs = (n - 1) * blk + kpos
        mask = (dist >= 0) & (dist < SWA_WINDOW) & (key_abs >= 0)
        distf = dist.astype(F32)
        for h in range(SWA_Q_HEADS):
            slope = 2.0 ** (-8.0 * (h + 1) / SWA_Q_HEADS)
            bias_ref[h] = jnp.where(mask, -slope * distf, NEG_INF)

    ones = jnp.ones((2 * blk, hd), BF16)
    low_half = lax.broadcasted_iota(jnp.int32, (blk, 2 * hd), 1) < hd
    half = SWA_GROUP // 2

    def kv_group(kv):
        ks = slice(kv * hd, (kv + 1) * hd)
        vs = slice(SWA_KV + kv * hd, SWA_KV + (kv + 1) * hd)
        kb = jnp.concatenate([kvp_ref[:, ks], kvc_ref[:, ks]], axis=0)
        vb = jnp.concatenate([kvp_ref[:, vs], kvc_ref[:, vs]], axis=0)
        v_even = jnp.concatenate([vb, ones], axis=1)
        v_odd = jnp.concatenate([ones, vb], axis=1)
        first = kv * SWA_GROUP
        heads = list(range(first, first + SWA_GROUP, 2)) + list(range(first + 1, first + SWA_GROUP, 2))
        qs = jnp.concatenate([q_ref[:, h * hd:(h + 1) * hd] for h in heads], axis=0)
        s = _dot_t1(qs * (hd ** -0.5), kb)
        yield

        def softmax_half(lo):
            ps, es = [], []
            for g in range(lo, lo + half):
                h = heads[g]
                sg = s[g * blk:(g + 1) * blk] + bias_ref[h]
                m = jnp.maximum(jnp.max(sg, axis=-1, keepdims=True), sinks_ref[h])
                ps.append(jnp.exp(sg - m).astype(BF16))
                es.append(jnp.exp(sinks_ref[h] - m))
            return jnp.concatenate(ps, axis=0), es

        p_even, e_even = softmax_half(0)
        pv_even = _dot(p_even, v_even)
        yield
        p_odd, e_odd = softmax_half(half)
        pv_odd = _dot(p_odd, v_odd)
        yield
        for t in range(half):
            rows = slice(t * blk, (t + 1) * blk)
            pe, po = pv_even[rows], pv_odd[rows]
            num = jnp.where(low_half, pe, po)
            den = pltpu.roll(jnp.where(low_half, po, pe), hd, axis=1)
            den = den + jnp.where(low_half, e_even[t], e_odd[t])
            pair = slice((first + 2 * t) * hd, (first + 2 * t + 2) * hd)
            o_ref[:, pair] = (num / den).astype(o_ref.dtype)

    groups = [kv_group(kv) for kv in range(SWA_KV_HEADS)]
    for kv in (0, 1, 0, 0, 2, 1, 0, 1, 2, 1, 2, 2):
        next(groups[kv], None)


def _swa_mixer(proj, sinks, batch, seq):
    t = proj.shape[0]
    nb = seq // SWA_BLOCK
    kvb = (SWA_Q + X_WIDTH) // IN_TILE
    return pl.pallas_call(
        _swa_kernel,
        out_shape=jax.ShapeDtypeStruct((t, SWA_Q), BF16),
        grid=(batch, nb),
        in_specs=[
            pl.BlockSpec(memory_space=pltpu.SMEM),
            pl.BlockSpec((SWA_BLOCK, SWA_Q), lambda b, n: (b * nb + n, 0)),
            pl.BlockSpec((SWA_BLOCK, IN_TILE),
                         lambda b, n: (b * nb + jnp.maximum(n - 1, 0), kvb)),
            pl.BlockSpec((SWA_BLOCK, IN_TILE), lambda b, n: (b * nb + n, kvb)),
        ],
        out_specs=pl.BlockSpec((SWA_BLOCK, SWA_Q), lambda b, n: (b * nb + n, 0)),
        scratch_shapes=[pltpu.VMEM((SWA_Q_HEADS, SWA_BLOCK, 2 * SWA_BLOCK), F32)],
        compiler_params=_params(("parallel", "arbitrary"), 32),
        name="swa_mixer",
    )(sinks, proj, proj, proj)


def _memattn_kernel(xq_ref, mk_ref, mv_ref, o_ref):
    hd = X_HEAD_DIM
    for h in range(X_HEADS):
        hs = slice(h * hd, (h + 1) * hd)
        s = _dot_t1(xq_ref[:, hs], mk_ref[:, hs]) * (hd ** -0.5)
        m = jnp.max(s, axis=-1, keepdims=True)
        p = jnp.exp(s - m)
        denom = jnp.sum(p, axis=-1, keepdims=True)
        oh = _dot(p.astype(BF16), mv_ref[:, hs]) / denom
        o_ref[:, hs] = oh.astype(o_ref.dtype)


def _mem_attention(proj, xq_block, mkv, layer, batch, seq, tm):
    t = proj.shape[0]
    ns = seq // tm
    return pl.pallas_call(
        _memattn_kernel,
        out_shape=jax.ShapeDtypeStruct((t, X_WIDTH), BF16),
        grid=(batch, ns),
        in_specs=[
            pl.BlockSpec((tm, X_WIDTH), lambda b, s: (b * ns + s, xq_block)),
            pl.BlockSpec((MEM_LEN, X_WIDTH), lambda b, s: (b, 2 * layer)),
            pl.BlockSpec((MEM_LEN, X_WIDTH), lambda b, s: (b, 2 * layer + 1)),
        ],
        out_specs=pl.BlockSpec((tm, X_WIDTH), lambda b, s: (b * ns + s, 0)),
        compiler_params=_params(("parallel", "arbitrary"), 32),
        name="mem_attention",
    )(proj, mkv, mkv)


def _out_proj_kernel(a_ref, b_ref, wa_ref, wb_ref, x_ref, o_ref):
    acc = _dot(a_ref[...], wa_ref[...].astype(BF16)) + _dot(b_ref[...], wb_ref[...].astype(BF16))
    o_ref[...] = x_ref[...] + acc


def _out_proj(o_mix, o_mem, w_out, layer, x, tm):
    t, d = x.shape
    ka, kb = o_mix.shape[1], o_mem.shape[1]
    once = pl.Buffered(1)
    return pl.pallas_call(
        _out_proj_kernel,
        out_shape=jax.ShapeDtypeStruct((t, d), F32),
        grid=(t // tm,),
        in_specs=[
            pl.BlockSpec((tm, ka), lambda i: (i, 0)),
            pl.BlockSpec((tm, kb), lambda i: (i, 0)),
            pl.BlockSpec((None, ka, d), lambda i: (layer, 0, 0), pipeline_mode=once),
            pl.BlockSpec((None, kb, d), lambda i: (layer, ka // kb, 0), pipeline_mode=once),
            pl.BlockSpec((tm, d), lambda i: (i, 0)),
        ],
        out_specs=pl.BlockSpec((tm, d), lambda i: (i, 0)),
        compiler_params=_params(("parallel",), 56),
        name="out_proj",
    )(o_mix, o_mem, w_out, w_out, x)


def _mlp_kernel(x_hbm, g_ref, wu_hbm, wd_hbm, gf_ref, o_ref, xbuf, xn_ref, wu_buf, wd_buf,
                sems, xsem, *, layer, final_norm):
    tf = wu_buf.shape[2]
    n_tiles = wu_hbm.shape[2] // tf

    def tile_copies(j, slot):
        off = pl.multiple_of(j * tf, tf)
        return (
            pltpu.make_async_copy(wu_hbm.at[layer, :, pl.ds(off, tf)], wu_buf.at[slot],
                                  sems.at[0, slot]),
            pltpu.make_async_copy(wd_hbm.at[layer, pl.ds(off, tf), :], wd_buf.at[slot],
                                  sems.at[1, slot]),
        )

    for cp in tile_copies(0, 0):
        cp.start()
    request_next_rows = _row_tile_fetch(x_hbm, xbuf, xsem.at[0])
    _rmsnorm_rows(xbuf, g_ref, xn_ref)
    o_ref[...] = xbuf[...]
    request_next_rows()

    def tile_pair(jj, carry):
        for slot in (0, 1):
            j = 2 * jj + slot

            @pl.when(j + 1 < n_tiles)
            def _():
                for cp in tile_copies(j + 1, 1 - slot):
                    cp.start()

            for cp in tile_copies(j, slot):
                cp.wait()
            h = _dot(xn_ref[...], wu_buf[slot].astype(BF16))
            h = jnp.square(jnp.maximum(h, 0.0)).astype(BF16)
            o_ref[...] += _dot(h, wd_buf[slot].astype(BF16))
        return carry

    lax.fori_loop(0, n_tiles // 2, tile_pair, 0)

    if final_norm:
        _rmsnorm_rows(o_ref, gf_ref, o_ref)


def _mlp(x, g, w_up, w_down, layer, g_final, final_norm, tm, tf):
    t, d = x.shape
    f = w_up.shape[2]
    assert f % (2 * tf) == 0
    return pl.pallas_call(
        functools.partial(_mlp_kernel, layer=layer, final_norm=final_norm),
        out_shape=jax.ShapeDtypeStruct((t, d), F32),
        grid=(t // tm,),
        in_specs=[
            pl.BlockSpec(memory_space=pl.ANY),
            pl.BlockSpec((1, d), lambda i: (0, 0)),
            pl.BlockSpec(memory_space=pl.ANY),
            pl.BlockSpec(memory_space=pl.ANY),
            pl.BlockSpec((1, d), lambda i: (0, 0)),
        ],
        out_specs=pl.BlockSpec((tm, d), lambda i: (i, 0)),
        scratch_shapes=[
            pltpu.VMEM((tm, d), F32),
            pltpu.VMEM((tm, d), BF16),
            pltpu.VMEM((2, d, tf), F32),
            pltpu.VMEM((2, tf, d), F32),
            pltpu.SemaphoreType.DMA((2, 2)),
            pltpu.SemaphoreType.DMA((1,)),
        ],
        compiler_params=_params(("arbitrary",), 58),
        name="mlp",
    )(x, g.reshape(1, d), w_up, w_down, g_final.reshape(1, d))


def _gla_tail_weight(w):
    o = 2 * GLA_K + 2 * GLA_V
    glr = w[:, o:o + GLA_GATE_RANK]
    xq = w[:, o + GLA_GATE_RANK:]
    pad = jnp.zeros((w.shape[0], GLA_RANK_PAD - GLA_GATE_RANK), w.dtype)
    return jnp.concatenate([xq, glr, pad], axis=1).astype(BF16)


def kernel(x, mem, mem_norm_g, attn_norm_g, w_in_gla, w_gate_up, b_gate, gla_out_norm_g,
           w_in_swa, sinks, w_mem_kv, w_out, mlp_norm_g, w_up, w_down, final_norm_g):
    batch, seq, d = x.shape
    t = batch * seq
    xf = x.reshape(t, d)
    assert X_WIDTH == IN_TILE and SWA_KV_HEADS == 3

    mkv = _norm_matmul(mem.reshape(batch * MEM_LEN, d), mem_norm_g, w_mem_kv,
                       tm=batch * MEM_LEN)

    for i in range(DEPTH):
        j = i // 2
        if i % 2 == 0:
            proj = _in_proj(xf, attn_norm_g[i], w_in_gla, j, GLA_PIECES, GLA_PROJ,
                            _gla_tail_weight(w_in_gla[j]), tm=1024)
            w_gate = jnp.pad(w_gate_up[j], ((0, GLA_RANK_PAD - GLA_GATE_RANK), (0, 0))).astype(BF16)
            o_mix = _gla_mixer(proj, w_gate, b_gate[j], gla_out_norm_g[j], batch, seq,
                               ts=1024, hpb=4)
            xq_block = (2 * GLA_V + 2 * GLA_K) // X_WIDTH
        else:
            proj = _in_proj(xf, attn_norm_g[i], w_in_swa, j, SWA_PIECES, SWA_PROJ, None, tm=1024)
            o_mix = _swa_mixer(proj, sinks[j], batch, seq)
            xq_block = SWA_Q // X_WIDTH
        o_mem = _mem_attention(proj, xq_block, mkv, i, batch, seq, tm=512)
        xf = _out_proj(o_mix, o_mem, w_out, i, xf, tm=512)
        xf = _mlp(xf, mlp_norm_g[i], w_up, w_down, i, final_norm_g,
                  final_norm=(i == DEPTH - 1), tm=1024, tf=512)
    return xf.reshape(batch, seq, d)
```

```python
import functools

import jax
import jax.numpy as jnp
from jax import lax
from jax.experimental import pallas as pl
from jax.experimental.pallas import tpu as pltpu

F32 = jnp.float32
BF16 = jnp.bfloat16

D_MODEL = 2048
DEPTH = 4
MEM_LEN = 256
RMS_EPS = 1e-5
NEG_INF = -1e30

X_HEADS = 4
X_HEAD_DIM = 128
X_WIDTH = X_HEADS * X_HEAD_DIM

GLA_HEADS = 4
GLA_DK = 256
GLA_DV = 384
GLA_K = GLA_HEADS * GLA_DK
GLA_V = GLA_HEADS * GLA_DV
GLA_GATE_RANK = 16
GLA_TAU = 16.0
GLA_CHUNK = 64
GLA_SUB = 256
GLA_RANK_PAD = 128
IN_TILE = 512
GLA_PROJ = 2 * GLA_V + 2 * GLA_K + X_WIDTH + GLA_RANK_PAD
GLA_PIECES = (
    tuple((2 * GLA_K + i * IN_TILE, i * IN_TILE, IN_TILE) for i in range(2 * GLA_V // IN_TILE))
    + tuple((i * IN_TILE, 2 * GLA_V + i * IN_TILE, IN_TILE) for i in range(2 * GLA_K // IN_TILE))
    + ((2 * GLA_K + 2 * GLA_V + GLA_GATE_RANK, 2 * GLA_K + 2 * GLA_V, X_WIDTH),
       (2 * GLA_K + 2 * GLA_V, 2 * GLA_K + 2 * GLA_V + X_WIDTH, GLA_GATE_RANK)))
GLA_CHUNKS = (0, 2048, 4096, GLA_PROJ)

SWA_HEAD_DIM = 64
SWA_Q_HEADS = 24
SWA_GROUP = 8
SWA_KV_HEADS = 3
SWA_WINDOW = 128
SWA_BLOCK = 128
SWA_Q = SWA_Q_HEADS * SWA_HEAD_DIM
SWA_KV = SWA_KV_HEADS * SWA_HEAD_DIM
SWA_PROJ = SWA_Q + X_WIDTH + IN_TILE
SWA_PIECES = (
    tuple((i * IN_TILE, i * IN_TILE, IN_TILE) for i in range(SWA_Q // IN_TILE))
    + ((SWA_Q + 2 * SWA_KV, SWA_Q, X_WIDTH), (SWA_Q, SWA_Q + X_WIDTH, IN_TILE)))
SWA_CHUNKS = (0, 1024, 2048, SWA_PROJ)

D_FF = 4 * D_MODEL

LANES = 128
BF16_SUBLANES = 16
MIB = 1024 * 1024


def _params(semantics, vmem_mib):
    return pltpu.CompilerParams(dimension_semantics=semantics,
                                vmem_limit_bytes=vmem_mib * MIB)


def _rmsnorm_rows(x_ref, g_ref, dst_ref, chunk=256):
    g = g_ref[...]

    def body(i, carry):
        r = pl.ds(pl.multiple_of(i * chunk, chunk), chunk)
        x = x_ref[r, :]
        ms = jnp.mean(x * x, axis=-1, keepdims=True)
        dst_ref[r, :] = (x * lax.rsqrt(ms + RMS_EPS) * g).astype(dst_ref.dtype)
        return carry

    lax.fori_loop(0, x_ref.shape[0] // chunk, body, 0)


def _dot(a, b):
    return jnp.dot(a, b, preferred_element_type=F32)


def _dot_t0(a, b):
    return lax.dot_general(a, b, (((0,), (0,)), ((), ())), preferred_element_type=F32)


def _dot_t1(a, b):
    return lax.dot_general(a, b, (((1,), (1,)), ((), ())), preferred_element_type=F32)


def _split_bf16(x):
    hi = x.astype(BF16)
    lo = (x - hi.astype(F32)).astype(BF16)
    return hi, lo


def _row_tiles(x_hbm, xbuf, sem):
    i = pl.program_id(0)
    tm = xbuf.shape[0]

    def copy(idx):
        rows = pl.ds(pl.multiple_of(idx * tm, tm), tm)
        return pltpu.make_async_copy(x_hbm.at[rows, :], xbuf, sem)

    def start_first():
        @pl.when(i == 0)
        def _():
            copy(0).start()

    def wait():
        copy(i).wait()

    def request_next():
        @pl.when(i + 1 < pl.num_programs(0))
        def _():
            copy(i + 1).start()

    return start_first, wait, request_next


def _norm_matmul_kernel(x_ref, g_ref, w_ref, o_ref, xn_ref):
    @pl.when(pl.program_id(1) == 0)
    def _():
        _rmsnorm_rows(x_ref, g_ref, xn_ref)

    o_ref[...] = _dot(xn_ref[...], w_ref[...].astype(BF16)).astype(o_ref.dtype)


def _norm_matmul(x, g, w, tm):
    t, d = x.shape
    layers, _, n1 = w.shape
    return pl.pallas_call(
        _norm_matmul_kernel,
        out_shape=jax.ShapeDtypeStruct((t, layers * n1), BF16),
        grid=(t // tm, layers),
        in_specs=[
            pl.BlockSpec((tm, d), lambda i, j: (i, 0)),
            pl.BlockSpec((1, d), lambda i, j: (0, 0)),
            pl.BlockSpec((None, d, n1), lambda i, j: (j, 0, 0)),
        ],
        out_specs=pl.BlockSpec((tm, n1), lambda i, j: (i, j)),
        scratch_shapes=[pltpu.VMEM((tm, d), BF16)],
        compiler_params=_params(("parallel", "arbitrary"), 52),
        name="norm_matmul",
    )(x, g.reshape(1, d), w)


def _in_proj_kernel(x_hbm, g_ref, w_hbm, o_ref, xbuf, xn_ref, wres, wbuf, sems,
                    *, layer, pieces, transposed, chunks):
    start_first_rows, wait_rows, request_next_rows = _row_tiles(x_hbm, xbuf, sems.at[0])
    start_first_rows()

    def piece_copy(p, slot):
        src, _, width = pieces[p]
        if transposed:
            return pltpu.make_async_copy(w_hbm.at[layer, pl.ds(src, width), :],
                                         wbuf.at[slot, pl.ds(0, width), :], sems.at[1 + slot])
        return pltpu.make_async_copy(w_hbm.at[layer, :, pl.ds(src, width)],
                                     wbuf.at[slot, :, pl.ds(0, width)], sems.at[1 + slot])

    @pl.when(pl.program_id(0) == 0)
    def _():
        piece_copy(0, 0).start()
        for p, (_, dst, width) in enumerate(pieces):
            slot = p % 2
            if p + 1 < len(pieces):
                piece_copy(p + 1, 1 - slot).start()
            piece_copy(p, slot).wait()
            if transposed:
                out_w = -(-width // LANES) * LANES
                tile = wbuf[slot, 0:out_w, :]
                if width < out_w:
                    r = lax.broadcasted_iota(jnp.int32, tile.shape, 0)
                    tile = jnp.where(r < width, tile, 0.0)
                wres[:, dst:dst + out_w] = tile.T.astype(BF16)
            else:
                wres[:, dst:dst + width] = wbuf[slot, :, 0:width].astype(BF16)

    wait_rows()
    _rmsnorm_rows(xbuf, g_ref, xn_ref)
    request_next_rows()
    for n0, n1 in zip(chunks[:-1], chunks[1:]):
        o_ref[:, n0:n1] = _dot(xn_ref[...], wres[:, n0:n1]).astype(o_ref.dtype)


def _in_proj(x, g, w, layer, pieces, chunks, transposed, tm):
    t, d = x.shape
    n_out = chunks[-1]
    stage = (2, IN_TILE, d) if transposed else (2, d, IN_TILE)
    return pl.pallas_call(
        functools.partial(_in_proj_kernel, layer=layer, pieces=pieces, transposed=transposed,
                          chunks=chunks),
        out_shape=jax.ShapeDtypeStruct((t, n_out), BF16),
        grid=(t // tm,),
        in_specs=[
            pl.BlockSpec(memory_space=pl.ANY),
            pl.BlockSpec((1, d), lambda i: (0, 0)),
            pl.BlockSpec(memory_space=pl.ANY),
        ],
        out_specs=pl.BlockSpec((tm, n_out), lambda i: (i, 0)),
        scratch_shapes=[
            pltpu.VMEM((tm, d), F32),
            pltpu.VMEM((tm, d), BF16),
            pltpu.VMEM((d, n_out), BF16),
            pltpu.VMEM(stage, F32),
            pltpu.SemaphoreType.DMA((3,)),
        ],
        compiler_params=_params(("arbitrary",), 60),
        name="in_proj",
    )(x, g.reshape(1, d), w)


def _gla_kernel(q_ref, k_ref, v_ref, go_ref, glr_ref, wg_ref, bg_ref, gn_ref,
                o_ref, state_ref):
    @pl.when(pl.program_id(2) == 0)
    def _():
        state_ref[...] = jnp.zeros_like(state_ref)

    c, sub = GLA_CHUNK, GLA_SUB
    nc = sub // c
    shift = c.bit_length() - 1
    row = lax.broadcasted_iota(jnp.int32, (sub, sub), 0)
    col = lax.broadcasted_iota(jnp.int32, (sub, sub), 1)
    causal = (jnp.right_shift(row, shift) == jnp.right_shift(col, shift)) & (row >= col)
    tri = jnp.where(causal, 1.0, 0.0).astype(BF16)
    prow = lax.broadcasted_iota(jnp.int32, (BF16_SUBLANES, nc * LANES), 0)
    pcol = lax.broadcasted_iota(jnp.int32, (BF16_SUBLANES, nc * LANES), 1)
    sel = jnp.where((prow < 2 * nc) & ((prow & (nc - 1)) == pcol // LANES), 1.0, 0.0).astype(BF16)
    gn = gn_ref[...]
    heads = q_ref.shape[1] // GLA_DK

    def head_sub_block(hh, r):
        ks = slice(hh * GLA_DK, (hh + 1) * GLA_DK)
        vs = slice(hh * GLA_DV, (hh + 1) * GLA_DV)
        pre = _dot(glr_ref[r, :], wg_ref[:, ks]) + bg_ref[:, ks]
        yield
        log_sig = jnp.minimum(pre, 0.0) - jnp.log(1.0 + jnp.exp(-jnp.abs(pre)))
        log_a = log_sig * (1.0 / GLA_TAU)
        la_hi, la_lo = _split_bf16(log_a)
        b = _dot(tri, la_hi) + _dot(tri, la_lo)
        yield
        b_last = [b[(i + 1) * c - 1:(i + 1) * c, :] for i in range(nc)]
        bl_hi, bl_lo = _split_bf16(jnp.concatenate(b_last, axis=0))
        pad = jnp.zeros((BF16_SUBLANES - 2 * nc, GLA_DK), BF16)
        b_last_cols = _dot_t0(jnp.concatenate([bl_hi, bl_lo, pad], axis=0), sel)
        b_last_rows = jnp.concatenate(
            [jnp.broadcast_to(x, (c, GLA_DK)) for x in b_last], axis=0)

        q = q_ref[r, ks].astype(F32) * (GLA_DK ** -0.5)
        k = k_ref[r, ks].astype(F32)
        v = v_ref[r, vs]
        q_in = (q * jnp.exp(b)).astype(BF16)
        k_in = (k * jnp.exp(-b)).astype(BF16)
        a_raw = _dot_t1(q_in, k_in)
        k_out = (k * jnp.exp(b_last_rows - b)).astype(BF16)
        kv = [_dot_t0(k_out[i * c:(i + 1) * c], v[i * c:(i + 1) * c]) for i in range(nc)]
        yield
        a = jnp.where(causal, a_raw, 0.0).astype(BF16)
        o_intra = _dot(a, v)
        decays = []
        for i in range(nc):
            decay = jnp.exp(b_last_cols[:, i * LANES:(i + 1) * LANES])
            decays.append(jnp.concatenate([decay] * (GLA_DV // LANES), axis=1))
        yield
        state = state_ref[hh]
        outs = []
        for i in range(nc):
            cs = slice(i * c, (i + 1) * c)
            outs.append(o_intra[cs] + _dot(q_in[cs], state.astype(BF16)))
            state = decays[i] * state + kv[i]
        state_ref[hh] = state
        yield
        o = jnp.concatenate(outs, axis=0)
        ms = jnp.mean(o * o, axis=-1, keepdims=True)
        y = o * lax.rsqrt(ms + RMS_EPS) * gn
        go = go_ref[r, vs].astype(F32)
        o_ref[r, vs] = (y * (go * jax.nn.sigmoid(go))).astype(o_ref.dtype)

    def sub_block(si, carry):
        r = pl.ds(pl.multiple_of(si * sub, sub), sub)
        pending = [head_sub_block(hh, r) for hh in range(heads)]
        while pending:
            pending = [gen for gen in pending if next(gen, True) is None]
        return carry

    lax.fori_loop(0, q_ref.shape[0] // sub, sub_block, 0)


def _gla_mixer(proj, w_gate, b_gate, out_g, batch, seq, ts, hpb):
    t = proj.shape[0]
    ns = seq // ts
    kw, vw = hpb * GLA_DK, hpb * GLA_DV
    vb, gb = 0, GLA_V // vw
    qb, kb = 2 * GLA_V // kw, (2 * GLA_V + GLA_K) // kw
    rb = (2 * GLA_V + 2 * GLA_K + X_WIDTH) // GLA_RANK_PAD

    def rows(b, h, c):
        return b * ns + c

    return pl.pallas_call(
        _gla_kernel,
        out_shape=jax.ShapeDtypeStruct((t, GLA_V), BF16),
        grid=(batch, GLA_HEADS // hpb, ns),
        in_specs=[
            pl.BlockSpec((ts, kw), lambda b, h, c: (rows(b, h, c), qb + h)),
            pl.BlockSpec((ts, kw), lambda b, h, c: (rows(b, h, c), kb + h)),
            pl.BlockSpec((ts, vw), lambda b, h, c: (rows(b, h, c), vb + h)),
            pl.BlockSpec((ts, vw), lambda b, h, c: (rows(b, h, c), gb + h)),
            pl.BlockSpec((ts, GLA_RANK_PAD), lambda b, h, c: (rows(b, h, c), rb)),
            pl.BlockSpec((GLA_RANK_PAD, kw), lambda b, h, c: (0, h)),
            pl.BlockSpec((1, kw), lambda b, h, c: (0, h)),
            pl.BlockSpec((1, GLA_DV), lambda b, h, c: (0, 0)),
        ],
        out_specs=pl.BlockSpec((ts, vw), lambda b, h, c: (rows(b, h, c), h)),
        scratch_shapes=[pltpu.VMEM((hpb, GLA_DK, GLA_DV), F32)],
        compiler_params=_params(("parallel", "parallel", "arbitrary"), 40),
        name="gla_mixer",
    )(proj, proj, proj, proj, proj, w_gate, b_gate.reshape(1, GLA_K),
      out_g.reshape(1, GLA_DV))


def _swa_kernel(sinks_ref, q_ref, kvp_ref, kvc_ref, o_ref, bias_ref):
    n = pl.program_id(1)
    blk = SWA_BLOCK
    hd = SWA_HEAD_DIM

    @pl.when(n <= 1)
    def _():
        qpos = lax.broadcasted_iota(jnp.int32, (blk, 2 * blk), 0) + blk
        kpos = lax.broadcasted_iota(jnp.int32, (blk, 2 * blk), 1)
        dist = qpos - kpos
        key_abs = (n - 1) * blk + kpos
        mask = (dist >= 0) & (dist < SWA_WINDOW) & (key_abs >= 0)
        distf = dist.astype(F32)
        for h in range(SWA_Q_HEADS):
            slope = 2.0 ** (-8.0 * (h + 1) / SWA_Q_HEADS)
            bias_ref[h] = jnp.where(mask, -slope * distf, NEG_INF)

    ones = jnp.ones((2 * blk, hd), BF16)
    low_half = lax.broadcasted_iota(jnp.int32, (blk, 2 * hd), 1) < hd
    half = SWA_GROUP // 2

    def kv_group(kv):
        ks = slice(kv * hd, (kv + 1) * hd)
        vs = slice(SWA_KV + kv * hd, SWA_KV + (kv + 1) * hd)
        kb = jnp.concatenate([kvp_ref[:, ks], kvc_ref[:, ks]], axis=0)
        vb = jnp.concatenate([kvp_ref[:, vs], kvc_ref[:, vs]], axis=0)
        v_even = jnp.concatenate([vb, ones], axis=1)
        v_odd = jnp.concatenate([ones, vb], axis=1)
        first = kv * SWA_GROUP
        heads = list(range(first, first + SWA_GROUP, 2)) + list(range(first + 1, first + SWA_GROUP, 2))
        qs = jnp.concatenate([q_ref[:, h * hd:(h + 1) * hd] for h in heads], axis=0)
        s = _dot_t1(qs * (hd ** -0.5), kb)
        yield

        def softmax_half(lo):
            ps, es = [], []
            for g in range(lo, lo + half):
                h = heads[g]
                sg = s[g * blk:(g + 1) * blk] + bias_ref[h]
                m = jnp.maximum(jnp.max(sg, axis=-1, keepdims=True), sinks_ref[h])
                ps.append(jnp.exp(sg - m).astype(BF16))
                es.append(jnp.exp(sinks_ref[h] - m))
            return jnp.concatenate(ps, axis=0), es

        p_even, e_even = softmax_half(0)
        pv_even = _dot(p_even, v_even)
        yield
        p_odd, e_odd = softmax_half(half)
        pv_odd = _dot(p_odd, v_odd)
        yield
        for t in range(half):
            rows = slice(t * blk, (t + 1) * blk)
            pe, po = pv_even[rows], pv_odd[rows]
            num = jnp.where(low_half, pe, po)
            den = pltpu.roll(jnp.where(low_half, po, pe), hd, axis=1)
            den = den + jnp.where(low_half, e_even[t], e_odd[t])
            pair = slice((first + 2 * t) * hd, (first + 2 * t + 2) * hd)
            o_ref[:, pair] = (num / den).astype(o_ref.dtype)

    groups = [kv_group(kv) for kv in range(SWA_KV_HEADS)]
    for kv in (0, 1, 0, 0, 2, 1, 0, 1, 2, 1, 2, 2):
        next(groups[kv], None)


def _swa_mixer(proj, sinks, batch, seq):
    t = proj.shape[0]
    nb = seq // SWA_BLOCK
    kvb = (SWA_Q + X_WIDTH) // IN_TILE
    return pl.pallas_call(
        _swa_kernel,
        out_shape=jax.ShapeDtypeStruct((t, SWA_Q), BF16),
        grid=(batch, nb),
        in_specs=[
            pl.BlockSpec(memory_space=pltpu.SMEM),
            pl.BlockSpec((SWA_BLOCK, SWA_Q), lambda b, n: (b * nb + n, 0)),
            pl.BlockSpec((SWA_BLOCK, IN_TILE),
                         lambda b, n: (b * nb + jnp.maximum(n - 1, 0), kvb)),
            pl.BlockSpec((SWA_BLOCK, IN_TILE), lambda b, n: (b * nb + n, kvb)),
        ],
        out_specs=pl.BlockSpec((SWA_BLOCK, SWA_Q), lambda b, n: (b * nb + n, 0)),
        scratch_shapes=[pltpu.VMEM((SWA_Q_HEADS, SWA_BLOCK, 2 * SWA_BLOCK), F32)],
        compiler_params=_params(("parallel", "arbitrary"), 32),
        name="swa_mixer",
    )(sinks, proj, proj, proj)


def _memattn_kernel(xq_ref, mk_ref, mv_ref, o_ref):
    hd = X_HEAD_DIM
    for h in range(X_HEADS):
        hs = slice(h * hd, (h + 1) * hd)
        s = _dot_t1(xq_ref[:, hs], mk_ref[:, hs]) * (hd ** -0.5)
        m = jnp.max(s, axis=-1, keepdims=True)
        p = jnp.exp(s - m)
        denom = jnp.sum(p, axis=-1, keepdims=True)
        oh = _dot(p.astype(BF16), mv_ref[:, hs]) / denom
        o_ref[:, hs] = oh.astype(o_ref.dtype)


def _mem_attention(proj, xq_block, mkv, layer, batch, seq, tm):
    t = proj.shape[0]
    ns = seq // tm
    return pl.pallas_call(
        _memattn_kernel,
        out_shape=jax.ShapeDtypeStruct((t, X_WIDTH), BF16),
        grid=(batch, ns),
        in_specs=[
            pl.BlockSpec((tm, X_WIDTH), lambda b, s: (b * ns + s, xq_block)),
            pl.BlockSpec((MEM_LEN, X_WIDTH), lambda b, s: (b, 2 * layer)),
            pl.BlockSpec((MEM_LEN, X_WIDTH), lambda b, s: (b, 2 * layer + 1)),
        ],
        out_specs=pl.BlockSpec((tm, X_WIDTH), lambda b, s: (b * ns + s, 0)),
        compiler_params=_params(("parallel", "arbitrary"), 32),
        name="mem_attention",
    )(proj, mkv, mkv)


def _out_proj_kernel(a_ref, b_ref, wa_ref, wb_ref, x_ref, o_ref):
    acc = _dot(a_ref[...], wa_ref[...].astype(BF16)) + _dot(b_ref[...], wb_ref[...].astype(BF16))
    o_ref[...] = x_ref[...] + acc


def _out_proj(o_mix, o_mem, w_out, layer, x, tm):
    t, d = x.shape
    ka, kb = o_mix.shape[1], o_mem.shape[1]
    once = pl.Buffered(1)
    return pl.pallas_call(
        _out_proj_kernel,
        out_shape=jax.ShapeDtypeStruct((t, d), F32),
        grid=(t // tm,),
        in_specs=[
            pl.BlockSpec((tm, ka), lambda i: (i, 0)),
            pl.BlockSpec((tm, kb), lambda i: (i, 0)),
            pl.BlockSpec((None, ka, d), lambda i: (layer, 0, 0), pipeline_mode=once),
            pl.BlockSpec((None, kb, d), lambda i: (layer, ka // kb, 0), pipeline_mode=once),
            pl.BlockSpec((tm, d), lambda i: (i, 0)),
        ],
        out_specs=pl.BlockSpec((tm, d), lambda i: (i, 0)),
        compiler_params=_params(("parallel",), 56),
        name="out_proj",
    )(o_mix, o_mem, w_out, w_out, x)


def _mlp_kernel(x_hbm, g_ref, wu_hbm, wd_hbm, gf_ref, o_ref, xbuf, xn_ref, wu_buf, wd_buf,
                sems, xsem, *, layer, final_norm):
    tf = wu_buf.shape[2]
    n_tiles = wu_hbm.shape[2] // tf

    def tile_copies(j, slot):
        off = pl.multiple_of(j * tf, tf)
        return (
            pltpu.make_async_copy(wu_hbm.at[layer, :, pl.ds(off, tf)], wu_buf.at[slot],
                                  sems.at[0, slot]),
            pltpu.make_async_copy(wd_hbm.at[layer, pl.ds(off, tf), :], wd_buf.at[slot],
                                  sems.at[1, slot]),
        )

    start_first_rows, wait_rows, request_next_rows = _row_tiles(x_hbm, xbuf, xsem.at[0])
    start_first_rows()
    for cp in tile_copies(0, 0):
        cp.start()
    wait_rows()
    _rmsnorm_rows(xbuf, g_ref, xn_ref)
    o_ref[...] = xbuf[...]
    request_next_rows()

    def tile_pair(jj, carry):
        for slot in (0, 1):
            j = 2 * jj + slot

            @pl.when(j + 1 < n_tiles)
            def _():
                for cp in tile_copies(j + 1, 1 - slot):
                    cp.start()

            for cp in tile_copies(j, slot):
                cp.wait()
            h = _dot(xn_ref[...], wu_buf[slot].astype(BF16))
            h = jnp.square(jnp.maximum(h, 0.0)).astype(BF16)
            o_ref[...] += _dot(h, wd_buf[slot].astype(BF16))
        return carry

    lax.fori_loop(0, n_tiles // 2, tile_pair, 0)

    if final_norm:
        _rmsnorm_rows(o_ref, gf_ref, o_ref)


def _mlp(x, g, w_up, w_down, layer, g_final, final_norm, tm, tf):
    t, d = x.shape
    f = w_up.shape[2]
    assert f % (2 * tf) == 0
    return pl.pallas_call(
        functools.partial(_mlp_kernel, layer=layer, final_norm=final_norm),
        out_shape=jax.ShapeDtypeStruct((t, d), F32),
        grid=(t // tm,),
        in_specs=[
            pl.BlockSpec(memory_space=pl.ANY),
            pl.BlockSpec((1, d), lambda i: (0, 0)),
            pl.BlockSpec(memory_space=pl.ANY),
            pl.BlockSpec(memory_space=pl.ANY),
            pl.BlockSpec((1, d), lambda i: (0, 0)),
        ],
        out_specs=pl.BlockSpec((tm, d), lambda i: (i, 0)),
        scratch_shapes=[
            pltpu.VMEM((tm, d), F32),
            pltpu.VMEM((tm, d), BF16),
            pltpu.VMEM((2, d, tf), F32),
            pltpu.VMEM((2, tf, d), F32),
            pltpu.SemaphoreType.DMA((2, 2)),
            pltpu.SemaphoreType.DMA((1,)),
        ],
        compiler_params=_params(("arbitrary",), 58),
        name="mlp",
    )(x, g.reshape(1, d), w_up, w_down, g_final.reshape(1, d))


def kernel(x, mem, mem_norm_g, attn_norm_g, w_in_gla, w_gate_up, b_gate, gla_out_norm_g,
           w_in_swa, sinks, w_mem_kv, w_out, mlp_norm_g, w_up, w_down, final_norm_g):
    batch, seq, d = x.shape
    t = batch * seq
    xf = x.reshape(t, d)
    assert X_WIDTH == IN_TILE and SWA_KV_HEADS == 3

    mkv = _norm_matmul(mem.reshape(batch * MEM_LEN, d), mem_norm_g, w_mem_kv,
                       tm=batch * MEM_LEN)

    w_in_gla_t = jnp.swapaxes(w_in_gla, 1, 2)

    for i in range(DEPTH):
        j = i // 2
        if i % 2 == 0:
            proj = _in_proj(xf, attn_norm_g[i], w_in_gla_t, j, GLA_PIECES, GLA_CHUNKS,
                            transposed=True, tm=512)
            w_gate = jnp.pad(w_gate_up[j], ((0, GLA_RANK_PAD - GLA_GATE_RANK), (0, 0))).astype(BF16)
            o_mix = _gla_mixer(proj, w_gate, b_gate[j], gla_out_norm_g[j], batch, seq,
                               ts=1024, hpb=4)
            xq_block = (2 * GLA_V + 2 * GLA_K) // X_WIDTH
        else:
            proj = _in_proj(xf, attn_norm_g[i], w_in_swa, j, SWA_PIECES, SWA_CHUNKS,
                            transposed=False, tm=512)
            o_mix = _swa_mixer(proj, sinks[j], batch, seq)
            xq_block = SWA_Q // X_WIDTH
        o_mem = _mem_attention(proj, xq_block, mkv, i, batch, seq, tm=512)
        xf = _out_proj(o_mix, o_mem, w_out, i, xf, tm=512)
        xf = _mlp(xf, mlp_norm_g[i], w_up, w_down, i, final_norm_g,
                  final_norm=(i == DEPTH - 1), tm=1024, tf=512)
    return xf.reshape(batch, seq, d)
```

```python
import functools

import jax
import jax.numpy as jnp
from jax import lax
from jax.experimental import pallas as pl
from jax.experimental.pallas import tpu as pltpu

F32 = jnp.float32
BF16 = jnp.bfloat16

D_MODEL = 2048
DEPTH = 4
MEM_LEN = 256
RMS_EPS = 1e-5
NEG_INF = -1e30

X_HEADS = 4
X_HEAD_DIM = 128
X_WIDTH = X_HEADS * X_HEAD_DIM

GLA_HEADS = 4
GLA_DK = 256
GLA_DV = 384
GLA_K = GLA_HEADS * GLA_DK
GLA_V = GLA_HEADS * GLA_DV
GLA_GATE_RANK = 16
GLA_TAU = 16.0
GLA_CHUNK = 64
GLA_SUB = 256
GLA_RANK_PAD = 128
IN_TILE = 512
GLA_PROJ = 2 * GLA_V + 2 * GLA_K + X_WIDTH + GLA_RANK_PAD
GLA_PIECES = (
    tuple((2 * GLA_K + i * IN_TILE, i * IN_TILE, IN_TILE) for i in range(2 * GLA_V // IN_TILE))
    + tuple((i * IN_TILE, 2 * GLA_V + i * IN_TILE, IN_TILE) for i in range(2 * GLA_K // IN_TILE))
    + ((2 * GLA_K + 2 * GLA_V + GLA_GATE_RANK, 2 * GLA_K + 2 * GLA_V, X_WIDTH),
       (2 * GLA_K + 2 * GLA_V, 2 * GLA_K + 2 * GLA_V + X_WIDTH, GLA_GATE_RANK)))
GLA_CHUNKS = (0, 2048, 4096, GLA_PROJ)

SWA_HEAD_DIM = 64
SWA_Q_HEADS = 24
SWA_GROUP = 8
SWA_KV_HEADS = 3
SWA_WINDOW = 128
SWA_BLOCK = 128
SWA_Q = SWA_Q_HEADS * SWA_HEAD_DIM
SWA_KV = SWA_KV_HEADS * SWA_HEAD_DIM
SWA_PROJ = SWA_Q + X_WIDTH + IN_TILE
SWA_PIECES = (
    tuple((i * IN_TILE, i * IN_TILE, IN_TILE) for i in range(SWA_Q // IN_TILE))
    + ((SWA_Q + 2 * SWA_KV, SWA_Q, X_WIDTH), (SWA_Q, SWA_Q + X_WIDTH, IN_TILE)))
SWA_CHUNKS = (0, 1024, 2048, SWA_PROJ)

D_FF = 4 * D_MODEL

LANES = 128
BF16_SUBLANES = 16
MIB = 1024 * 1024


def _params(semantics, vmem_mib):
    return pltpu.CompilerParams(dimension_semantics=semantics,
                                vmem_limit_bytes=vmem_mib * MIB)


def _rmsnorm_rows(x_ref, g_ref, dst_ref, chunk=256):
    g = g_ref[...]

    def body(i, carry):
        r = pl.ds(pl.multiple_of(i * chunk, chunk), chunk)
        x = x_ref[r, :]
        ms = jnp.mean(x * x, axis=-1, keepdims=True)
        dst_ref[r, :] = (x * lax.rsqrt(ms + RMS_EPS) * g).astype(dst_ref.dtype)
        return carry

    lax.fori_loop(0, x_ref.shape[0] // chunk, body, 0)


def _rmsnorm_piece(x_ref, rows, g, dst_ref, sub=128):
    for r0 in range(0, rows.size, sub):
        r = pl.ds(rows.start + r0, sub)
        x = x_ref[r, :]
        ms = jnp.mean(x * x, axis=-1, keepdims=True)
        dst_ref[r, :] = (x * lax.rsqrt(ms + RMS_EPS) * g).astype(dst_ref.dtype)


def _by_parity(i, body, ref_a, ref_b):
    @pl.when(lax.rem(i, 2) == 0)
    def _():
        body(ref_a, ref_b)

    @pl.when(lax.rem(i, 2) == 1)
    def _():
        body(ref_b, ref_a)


def _dot(a, b):
    return jnp.dot(a, b, preferred_element_type=F32)


def _dot_t0(a, b):
    return lax.dot_general(a, b, (((0,), (0,)), ((), ())), preferred_element_type=F32)


def _dot_t1(a, b):
    return lax.dot_general(a, b, (((1,), (1,)), ((), ())), preferred_element_type=F32)


def _split_bf16(x):
    hi = x.astype(BF16)
    lo = (x - hi.astype(F32)).astype(BF16)
    return hi, lo


def _row_tile_copy(x_hbm, xbuf, sem):
    tm = xbuf.shape[0]

    def copy(idx):
        rows = pl.ds(pl.multiple_of(idx * tm, tm), tm)
        return pltpu.make_async_copy(x_hbm.at[rows, :], xbuf, sem)

    return copy


def _norm_matmul_kernel(x_ref, g_ref, w_ref, o_ref, xn_ref):
    @pl.when(pl.program_id(1) == 0)
    def _():
        _rmsnorm_rows(x_ref, g_ref, xn_ref)

    o_ref[...] = _dot(xn_ref[...], w_ref[...].astype(BF16)).astype(o_ref.dtype)


def _norm_matmul(x, g, w, tm):
    t, d = x.shape
    layers, _, n1 = w.shape
    return pl.pallas_call(
        _norm_matmul_kernel,
        out_shape=jax.ShapeDtypeStruct((t, layers * n1), BF16),
        grid=(t // tm, layers),
        in_specs=[
            pl.BlockSpec((tm, d), lambda i, j: (i, 0)),
            pl.BlockSpec((1, d), lambda i, j: (0, 0)),
            pl.BlockSpec((None, d, n1), lambda i, j: (j, 0, 0)),
        ],
        out_specs=pl.BlockSpec((tm, n1), lambda i, j: (i, j)),
        scratch_shapes=[pltpu.VMEM((tm, d), BF16)],
        compiler_params=_params(("parallel", "arbitrary"), 52),
        name="norm_matmul",
    )(x, g.reshape(1, d), w)


def _in_proj_kernel(x_hbm, g_ref, w_hbm, o_ref, xbuf, xn_a, xn_b, wres, wbuf, sems,
                    *, layer, pieces, transposed, chunks, n_steps):
    i = pl.program_id(0)
    x_copy = _row_tile_copy(x_hbm, xbuf, sems.at[0])

    def piece_copy(p, slot):
        src, _, width = pieces[p]
        if transposed:
            return pltpu.make_async_copy(w_hbm.at[layer, pl.ds(src, width), :],
                                         wbuf.at[slot, pl.ds(0, width), :], sems.at[1 + slot])
        return pltpu.make_async_copy(w_hbm.at[layer, :, pl.ds(src, width)],
                                     wbuf.at[slot, :, pl.ds(0, width)], sems.at[1 + slot])

    @pl.when(i == 0)
    def _():
        x_copy(0).start()
        piece_copy(0, 0).start()
        for p, (_, dst, width) in enumerate(pieces):
            slot = p % 2
            if p + 1 < len(pieces):
                piece_copy(p + 1, 1 - slot).start()
            piece_copy(p, slot).wait()
            if transposed:
                out_w = -(-width // LANES) * LANES
                tile = wbuf[slot, 0:out_w, :]
                if width < out_w:
                    r = lax.broadcasted_iota(jnp.int32, tile.shape, 0)
                    tile = jnp.where(r < width, tile, 0.0)
                wres[:, dst:dst + out_w] = tile.T.astype(BF16)
            else:
                wres[:, dst:dst + width] = wbuf[slot, :, 0:width].astype(BF16)
        x_copy(0).wait()
        _rmsnorm_rows(xbuf, g_ref, xn_a)
        if n_steps > 1:
            x_copy(1).start()

    @pl.when(i + 1 < n_steps)
    def _():
        x_copy(i + 1).wait()

    n_chunks = len(chunks) - 1
    tm = xbuf.shape[0]
    edges = [(tm // LANES) * c // n_chunks * LANES for c in range(n_chunks + 1)]

    def multiply(cur, nxt):
        g = g_ref[...]
        for c in range(n_chunks):
            if edges[c + 1] > edges[c]:
                _rmsnorm_piece(xbuf, pl.ds(edges[c], edges[c + 1] - edges[c]), g, nxt)
            n0, n1 = chunks[c], chunks[c + 1]
            o_ref[:, n0:n1] = _dot(cur[...], wres[:, n0:n1]).astype(o_ref.dtype)

    _by_parity(i, multiply, xn_a, xn_b)

    @pl.when(i + 2 < n_steps)
    def _():
        x_copy(i + 2).start()


def _in_proj(x, g, w, layer, pieces, chunks, transposed, tm):
    t, d = x.shape
    n_out = chunks[-1]
    stage = (2, IN_TILE, d) if transposed else (2, d, IN_TILE)
    return pl.pallas_call(
        functools.partial(_in_proj_kernel, layer=layer, pieces=pieces, transposed=transposed,
                          chunks=chunks, n_steps=t // tm),
        out_shape=jax.ShapeDtypeStruct((t, n_out), BF16),
        grid=(t // tm,),
        in_specs=[
            pl.BlockSpec(memory_space=pl.ANY),
            pl.BlockSpec((1, d), lambda i: (0, 0)),
            pl.BlockSpec(memory_space=pl.ANY),
        ],
        out_specs=pl.BlockSpec((tm, n_out), lambda i: (i, 0)),
        scratch_shapes=[
            pltpu.VMEM((tm, d), F32),
            pltpu.VMEM((tm, d), BF16),
            pltpu.VMEM((tm, d), BF16),
            pltpu.VMEM((d, n_out), BF16),
            pltpu.VMEM(stage, F32),
            pltpu.SemaphoreType.DMA((3,)),
        ],
        compiler_params=_params(("arbitrary",), 60),
        name="in_proj",
    )(x, g.reshape(1, d), w)


def _gla_kernel(q_ref, k_ref, v_ref, go_ref, glr_ref, wg_ref, bg_ref, gn_ref,
                o_ref, state_ref):
    @pl.when(pl.program_id(2) == 0)
    def _():
        state_ref[...] = jnp.zeros_like(state_ref)

    c, sub = GLA_CHUNK, GLA_SUB
    nc = sub // c
    shift = c.bit_length() - 1
    row = lax.broadcasted_iota(jnp.int32, (sub, sub), 0)
    col = lax.broadcasted_iota(jnp.int32, (sub, sub), 1)
    causal = (jnp.right_shift(row, shift) == jnp.right_shift(col, shift)) & (row >= col)
    tri = jnp.where(causal, 1.0, 0.0).astype(BF16)
    prow = lax.broadcasted_iota(jnp.int32, (BF16_SUBLANES, nc * LANES), 0)
    pcol = lax.broadcasted_iota(jnp.int32, (BF16_SUBLANES, nc * LANES), 1)
    sel = jnp.where((prow < 2 * nc) & ((prow & (nc - 1)) == pcol // LANES), 1.0, 0.0).astype(BF16)
    gn = gn_ref[...]
    heads = q_ref.shape[1] // GLA_DK

    def head_sub_block(hh, r):
        ks = slice(hh * GLA_DK, (hh + 1) * GLA_DK)
        vs = slice(hh * GLA_DV, (hh + 1) * GLA_DV)
        pre = _dot(glr_ref[r, :], wg_ref[:, ks]) + bg_ref[:, ks]
        yield
        log_sig = jnp.minimum(pre, 0.0) - jnp.log(1.0 + jnp.exp(-jnp.abs(pre)))
        log_a = log_sig * (1.0 / GLA_TAU)
        la_hi, la_lo = _split_bf16(log_a)
        b = _dot(tri, la_hi) + _dot(tri, la_lo)
        yield
        b_last = [b[(i + 1) * c - 1:(i + 1) * c, :] for i in range(nc)]
        bl_hi, bl_lo = _split_bf16(jnp.concatenate(b_last, axis=0))
        pad = jnp.zeros((BF16_SUBLANES - 2 * nc, GLA_DK), BF16)
        b_last_cols = _dot_t0(jnp.concatenate([bl_hi, bl_lo, pad], axis=0), sel)
        b_last_rows = jnp.concatenate(
            [jnp.broadcast_to(x, (c, GLA_DK)) for x in b_last], axis=0)

        q = q_ref[r, ks].astype(F32) * (GLA_DK ** -0.5)
        k = k_ref[r, ks].astype(F32)
        v = v_ref[r, vs]
        q_in = (q * jnp.exp(b)).astype(BF16)
        k_in = (k * jnp.exp(-b)).astype(BF16)
        a_raw = _dot_t1(q_in, k_in)
        k_out = (k * jnp.exp(b_last_rows - b)).astype(BF16)
        kv = [_dot_t0(k_out[i * c:(i + 1) * c], v[i * c:(i + 1) * c]) for i in range(nc)]
        yield
        a = jnp.where(causal, a_raw, 0.0).astype(BF16)
        o_intra = _dot(a, v)
        decays = []
        for i in range(nc):
            decay = jnp.exp(b_last_cols[:, i * LANES:(i + 1) * LANES])
            decays.append(jnp.concatenate([decay] * (GLA_DV // LANES), axis=1))
        yield
        state = state_ref[hh]
        outs = []
        for i in range(nc):
            cs = slice(i * c, (i + 1) * c)
            outs.append(o_intra[cs] + _dot(q_in[cs], state.astype(BF16)))
            state = decays[i] * state + kv[i]
        state_ref[hh] = state
        yield
        o = jnp.concatenate(outs, axis=0)
        ms = jnp.mean(o * o, axis=-1, keepdims=True)
        y = o * lax.rsqrt(ms + RMS_EPS) * gn
        go = go_ref[r, vs].astype(F32)
        o_ref[r, vs] = (y * (go * jax.nn.sigmoid(go))).astype(o_ref.dtype)

    def sub_block(si, carry):
        r = pl.ds(pl.multiple_of(si * sub, sub), sub)
        pending = [head_sub_block(hh, r) for hh in range(heads)]
        while pending:
            pending = [gen for gen in pending if next(gen, True) is None]
        return carry

    lax.fori_loop(0, q_ref.shape[0] // sub, sub_block, 0)


def _gla_mixer(proj, w_gate, b_gate, out_g, batch, seq, ts, hpb):
    t = proj.shape[0]
    ns = seq // ts
    kw, vw = hpb * GLA_DK, hpb * GLA_DV
    vb, gb = 0, GLA_V // vw
    qb, kb = 2 * GLA_V // kw, (2 * GLA_V + GLA_K) // kw
    rb = (2 * GLA_V + 2 * GLA_K + X_WIDTH) // GLA_RANK_PAD

    def rows(b, h, c):
        return b * ns + c

    return pl.pallas_call(
        _gla_kernel,
        out_shape=jax.ShapeDtypeStruct((t, GLA_V), BF16),
        grid=(batch, GLA_HEADS // hpb, ns),
        in_specs=[
            pl.BlockSpec((ts, kw), lambda b, h, c: (rows(b, h, c), qb + h)),
            pl.BlockSpec((ts, kw), lambda b, h, c: (rows(b, h, c), kb + h)),
            pl.BlockSpec((ts, vw), lambda b, h, c: (rows(b, h, c), vb + h)),
            pl.BlockSpec((ts, vw), lambda b, h, c: (rows(b, h, c), gb + h)),
            pl.BlockSpec((ts, GLA_RANK_PAD), lambda b, h, c: (rows(b, h, c), rb)),
            pl.BlockSpec((GLA_RANK_PAD, kw), lambda b, h, c: (0, h)),
            pl.BlockSpec((1, kw), lambda b, h, c: (0, h)),
            pl.BlockSpec((1, GLA_DV), lambda b, h, c: (0, 0)),
        ],
        out_specs=pl.BlockSpec((ts, vw), lambda b, h, c: (rows(b, h, c), h)),
        scratch_shapes=[pltpu.VMEM((hpb, GLA_DK, GLA_DV), F32)],
        compiler_params=_params(("parallel", "parallel", "arbitrary"), 40),
        name="gla_mixer",
    )(proj, proj, proj, proj, proj, w_gate, b_gate.reshape(1, GLA_K),
      out_g.reshape(1, GLA_DV))


def _swa_kernel(sinks_ref, q_ref, kvp_ref, kvc_ref, o_ref, bias_ref):
    n = pl.program_id(1)
    blk = SWA_BLOCK
    hd = SWA_HEAD_DIM

    @pl.when(n <= 1)
    def _():
        qpos = lax.broadcasted_iota(jnp.int32, (blk, 2 * blk), 0) + blk
        kpos = lax.broadcasted_iota(jnp.int32, (blk, 2 * blk), 1)
        dist = qpos - kpos
        key_abs = (n - 1) * blk + kpos
        mask = (dist >= 0) & (dist < SWA_WINDOW) & (key_abs >= 0)
        distf = dist.astype(F32)
        for h in range(SWA_Q_HEADS):
            slope = 2.0 ** (-8.0 * (h + 1) / SWA_Q_HEADS)
            bias_ref[h] = jnp.where(mask, -slope * distf, NEG_INF)

    ones = jnp.ones((2 * blk, hd), BF16)
    low_half = lax.broadcasted_iota(jnp.int32, (blk, 2 * hd), 1) < hd
    half = SWA_GROUP // 2

    def kv_group(kv):
        ks = slice(kv * hd, (kv + 1) * hd)
        vs = slice(SWA_KV + kv * hd, SWA_KV + (kv + 1) * hd)
        kb = jnp.concatenate([kvp_ref[:, ks], kvc_ref[:, ks]], axis=0)
        vb = jnp.concatenate([kvp_ref[:, vs], kvc_ref[:, vs]], axis=0)
        v_even = jnp.concatenate([vb, ones], axis=1)
        v_odd = jnp.concatenate([ones, vb], axis=1)
        first = kv * SWA_GROUP
        heads = list(range(first, first + SWA_GROUP, 2)) + list(range(first + 1, first + SWA_GROUP, 2))
        qs = jnp.concatenate([q_ref[:, h * hd:(h + 1) * hd] for h in heads], axis=0)
        s = _dot_t1(qs * (hd ** -0.5), kb)
        yield

        def softmax_half(lo):
            ps, es = [], []
            for g in range(lo, lo + half):
                h = heads[g]
                sg = s[g * blk:(g + 1) * blk] + bias_ref[h]
                m = jnp.maximum(jnp.max(sg, axis=-1, keepdims=True), sinks_ref[h])
                ps.append(jnp.exp(sg - m).astype(BF16))
                es.append(jnp.exp(sinks_ref[h] - m))
            return jnp.concatenate(ps, axis=0), es

        p_even, e_even = softmax_half(0)
        pv_even = _dot(p_even, v_even)
        yield
        p_odd, e_odd = softmax_half(half)
        pv_odd = _dot(p_odd, v_odd)
        yield
        for t in range(half):
            rows = slice(t * blk, (t + 1) * blk)
            pe, po = pv_even[rows], pv_odd[rows]
            num = jnp.where(low_half, pe, po)
            den = pltpu.roll(jnp.where(low_half, po, pe), hd, axis=1)
            den = den + jnp.where(low_half, e_even[t], e_odd[t])
            pair = slice((first + 2 * t) * hd, (first + 2 * t + 2) * hd)
            o_ref[:, pair] = (num / den).astype(o_ref.dtype)

    groups = [kv_group(kv) for kv in range(SWA_KV_HEADS)]
    for kv in (0, 1, 0, 0, 2, 1, 0, 1, 2, 1, 2, 2):
        next(groups[kv], None)


def _swa_mixer(proj, sinks, batch, seq):
    t = proj.shape[0]
    nb = seq // SWA_BLOCK
    kvb = (SWA_Q + X_WIDTH) // IN_TILE
    return pl.pallas_call(
        _swa_kernel,
        out_shape=jax.ShapeDtypeStruct((t, SWA_Q), BF16),
        grid=(batch, nb),
        in_specs=[
            pl.BlockSpec(memory_space=pltpu.SMEM),
            pl.BlockSpec((SWA_BLOCK, SWA_Q), lambda b, n: (b * nb + n, 0)),
            pl.BlockSpec((SWA_BLOCK, IN_TILE),
                         lambda b, n: (b * nb + jnp.maximum(n - 1, 0), kvb)),
            pl.BlockSpec((SWA_BLOCK, IN_TILE), lambda b, n: (b * nb + n, kvb)),
        ],
        out_specs=pl.BlockSpec((SWA_BLOCK, SWA_Q), lambda b, n: (b * nb + n, 0)),
        scratch_shapes=[pltpu.VMEM((SWA_Q_HEADS, SWA_BLOCK, 2 * SWA_BLOCK), F32)],
        compiler_params=_params(("parallel", "arbitrary"), 32),
        name="swa_mixer",
    )(sinks, proj, proj, proj)


def _memattn_kernel(xq_ref, mk_ref, mv_ref, o_ref):
    hd = X_HEAD_DIM
    for h in range(X_HEADS):
        hs = slice(h * hd, (h + 1) * hd)
        s = _dot_t1(xq_ref[:, hs], mk_ref[:, hs]) * (hd ** -0.5)
        m = jnp.max(s, axis=-1, keepdims=True)
        p = jnp.exp(s - m)
        denom = jnp.sum(p, axis=-1, keepdims=True)
        oh = _dot(p.astype(BF16), mv_ref[:, hs]) / denom
        o_ref[:, hs] = oh.astype(o_ref.dtype)


def _mem_attention(proj, xq_block, mkv, layer, batch, seq, tm):
    t = proj.shape[0]
    ns = seq // tm
    return pl.pallas_call(
        _memattn_kernel,
        out_shape=jax.ShapeDtypeStruct((t, X_WIDTH), BF16),
        grid=(batch, ns),
        in_specs=[
            pl.BlockSpec((tm, X_WIDTH), lambda b, s: (b * ns + s, xq_block)),
            pl.BlockSpec((MEM_LEN, X_WIDTH), lambda b, s: (b, 2 * layer)),
            pl.BlockSpec((MEM_LEN, X_WIDTH), lambda b, s: (b, 2 * layer + 1)),
        ],
        out_specs=pl.BlockSpec((tm, X_WIDTH), lambda b, s: (b * ns + s, 0)),
        compiler_params=_params(("parallel", "arbitrary"), 32),
        name="mem_attention",
    )(proj, mkv, mkv)


def _out_proj_kernel(a_ref, b_ref, wa_ref, wb_ref, x_ref, o_ref):
    acc = _dot(a_ref[...], wa_ref[...].astype(BF16)) + _dot(b_ref[...], wb_ref[...].astype(BF16))
    o_ref[...] = x_ref[...] + acc


def _out_proj(o_mix, o_mem, w_out, layer, x, tm):
    t, d = x.shape
    ka, kb = o_mix.shape[1], o_mem.shape[1]
    once = pl.Buffered(1)
    return pl.pallas_call(
        _out_proj_kernel,
        out_shape=jax.ShapeDtypeStruct((t, d), F32),
        grid=(t // tm,),
        in_specs=[
            pl.BlockSpec((tm, ka), lambda i: (i, 0)),
            pl.BlockSpec((tm, kb), lambda i: (i, 0)),
            pl.BlockSpec((None, ka, d), lambda i: (layer, 0, 0), pipeline_mode=once),
            pl.BlockSpec((None, kb, d), lambda i: (layer, ka // kb, 0), pipeline_mode=once),
            pl.BlockSpec((tm, d), lambda i: (i, 0)),
        ],
        out_specs=pl.BlockSpec((tm, d), lambda i: (i, 0)),
        compiler_params=_params(("parallel",), 56),
        name="out_proj",
    )(o_mix, o_mem, w_out, w_out, x)


def _mlp_kernel(x_hbm, g_ref, wu_hbm, wd_hbm, gf_ref, o_ref, xbuf, xn_a, xn_b, wu_buf, wd_buf,
                sems, xsem, *, layer, final_norm, n_steps):
    i = pl.program_id(0)
    tm = xbuf.shape[0]
    tf = wu_buf.shape[2]
    n_tiles = wu_hbm.shape[2] // tf
    n_pairs = n_tiles // 2
    piece = tm // n_pairs
    x_copy = _row_tile_copy(x_hbm, xbuf, xsem.at[0])

    def tile_copies(j, slot):
        off = pl.multiple_of(j * tf, tf)
        return (
            pltpu.make_async_copy(wu_hbm.at[layer, :, pl.ds(off, tf)], wu_buf.at[slot],
                                  sems.at[0, slot]),
            pltpu.make_async_copy(wd_hbm.at[layer, pl.ds(off, tf), :], wd_buf.at[slot],
                                  sems.at[1, slot]),
        )

    for cp in tile_copies(0, 0):
        cp.start()

    @pl.when(i == 0)
    def _():
        x_copy(0).start()
        x_copy(0).wait()
        _rmsnorm_rows(xbuf, g_ref, xn_a)

    o_ref[...] = xbuf[...]
    has_next = i + 1 < n_steps

    @pl.when(has_next)
    def _():
        x_copy(i + 1).start()

    def walk_tiles(cur, nxt):
        g = g_ref[...]

        def tile_pair(jj, carry):
            for slot in (0, 1):
                j = 2 * jj + slot

                @pl.when(j + 1 < n_tiles)
                def _():
                    for cp in tile_copies(j + 1, 1 - slot):
                        cp.start()

                if slot == 1:
                    @pl.when(jnp.logical_and(jj == 0, has_next))
                    def _():
                        x_copy(i + 1).wait()

                for cp in tile_copies(j, slot):
                    cp.wait()
                if slot == 1:
                    rows = pl.ds(pl.multiple_of(jj * piece, piece), piece)
                    _rmsnorm_piece(xbuf, rows, g, nxt)
                h = _dot(cur[...], wu_buf[slot].astype(BF16))
                h = jnp.square(jnp.maximum(h, 0.0)).astype(BF16)
                o_ref[...] += _dot(h, wd_buf[slot].astype(BF16))
            return carry

        lax.fori_loop(0, n_pairs, tile_pair, 0)

    _by_parity(i, walk_tiles, xn_a, xn_b)

    if final_norm:
        _rmsnorm_rows(o_ref, gf_ref, o_ref)


def _mlp(x, g, w_up, w_down, layer, g_final, final_norm, tm, tf):
    t, d = x.shape
    f = w_up.shape[2]
    assert f % (2 * tf) == 0
    return pl.pallas_call(
        functools.partial(_mlp_kernel, layer=layer, final_norm=final_norm, n_steps=t // tm),
        out_shape=jax.ShapeDtypeStruct((t, d), F32),
        grid=(t // tm,),
        in_specs=[
            pl.BlockSpec(memory_space=pl.ANY),
            pl.BlockSpec((1, d), lambda i: (0, 0)),
            pl.BlockSpec(memory_space=pl.ANY),
            pl.BlockSpec(memory_space=pl.ANY),
            pl.BlockSpec((1, d), lambda i: (0, 0)),
        ],
        out_specs=pl.BlockSpec((tm, d), lambda i: (i, 0)),
        scratch_shapes=[
            pltpu.VMEM((tm, d), F32),
            pltpu.VMEM((tm, d), BF16),
            pltpu.VMEM((tm, d), BF16),
            pltpu.VMEM((2, d, tf), F32),
            pltpu.VMEM((2, tf, d), F32),
            pltpu.SemaphoreType.DMA((2, 2)),
            pltpu.SemaphoreType.DMA((1,)),
        ],
        compiler_params=_params(("arbitrary",), 58),
        name="mlp",
    )(x, g.reshape(1, d), w_up, w_down, g_final.reshape(1, d))


def kernel(x, mem, mem_norm_g, attn_norm_g, w_in_gla, w_gate_up, b_gate, gla_out_norm_g,
           w_in_swa, sinks, w_mem_kv, w_out, mlp_norm_g, w_up, w_down, final_norm_g):
    batch, seq, d = x.shape
    t = batch * seq
    xf = x.reshape(t, d)
    assert X_WIDTH == IN_TILE and SWA_KV_HEADS == 3

    mkv = _norm_matmul(mem.reshape(batch * MEM_LEN, d), mem_norm_g, w_mem_kv,
                       tm=batch * MEM_LEN)

    w_in_gla_t = jnp.swapaxes(w_in_gla, 1, 2)

    for i in range(DEPTH):
        j = i // 2
        if i % 2 == 0:
            proj = _in_proj(xf, attn_norm_g[i], w_in_gla_t, j, GLA_PIECES, GLA_CHUNKS,
                            transposed=True, tm=512)
            w_gate = jnp.pad(w_gate_up[j], ((0, GLA_RANK_PAD - GLA_GATE_RANK), (0, 0))).astype(BF16)
            o_mix = _gla_mixer(proj, w_gate, b_gate[j], gla_out_norm_g[j], batch, seq,
                               ts=1024, hpb=4)
            xq_block = (2 * GLA_V + 2 * GLA_K) // X_WIDTH
        else:
            proj = _in_proj(xf, attn_norm_g[i], w_in_swa, j, SWA_PIECES, SWA_CHUNKS,
                            transposed=False, tm=512)
            o_mix = _swa_mixer(proj, sinks[j], batch, seq)
            xq_block = SWA_Q // X_WIDTH
        o_mem = _mem_attention(proj, xq_block, mkv, i, batch, seq, tm=512)
        xf = _out_proj(o_mix, o_mem, w_out, i, xf, tm=512)
        xf = _mlp(xf, mlp_norm_g[i], w_up, w_down, i, final_norm_g,
                  final_norm=(i == DEPTH - 1), tm=1024, tf=512)
    return xf.reshape(batch, seq, d)
```

```python
import functools

import jax
import jax.numpy as jnp
from jax import lax
from jax.experimental import pallas as pl
from jax.experimental.pallas import tpu as pltpu

F32 = jnp.float32
BF16 = jnp.bfloat16

D_MODEL = 2048
DEPTH = 4
MEM_LEN = 256
RMS_EPS = 1e-5
NEG_INF = -1e30

X_HEADS = 4
X_HEAD_DIM = 128
X_WIDTH = X_HEADS * X_HEAD_DIM

GLA_HEADS = 4
GLA_DK = 256
GLA_DV = 384
GLA_K = GLA_HEADS * GLA_DK
GLA_V = GLA_HEADS * GLA_DV
GLA_GATE_RANK = 16
GLA_TAU = 16.0
GLA_CHUNK = 64
GLA_SUB = 256
GLA_RANK_PAD = 128
IN_TILE = 512
GLA_PROJ = 2 * GLA_V + 2 * GLA_K + X_WIDTH + GLA_RANK_PAD
GLA_PIECES = (
    tuple((2 * GLA_K + i * IN_TILE, i * IN_TILE, IN_TILE) for i in range(2 * GLA_V // IN_TILE))
    + tuple((i * IN_TILE, 2 * GLA_V + i * IN_TILE, IN_TILE) for i in range(2 * GLA_K // IN_TILE))
    + ((2 * GLA_K + 2 * GLA_V + GLA_GATE_RANK, 2 * GLA_K + 2 * GLA_V, X_WIDTH),
       (2 * GLA_K + 2 * GLA_V, 2 * GLA_K + 2 * GLA_V + X_WIDTH, GLA_GATE_RANK)))
GLA_CHUNKS = (0, 2048, 4096, GLA_PROJ)

SWA_HEAD_DIM = 64
SWA_Q_HEADS = 24
SWA_GROUP = 8
SWA_KV_HEADS = 3
SWA_WINDOW = 128
SWA_BLOCK = 128
SWA_Q = SWA_Q_HEADS * SWA_HEAD_DIM
SWA_KV = SWA_KV_HEADS * SWA_HEAD_DIM
SWA_PROJ = SWA_Q + X_WIDTH + IN_TILE
SWA_PIECES = (
    tuple((i * IN_TILE, i * IN_TILE, IN_TILE) for i in range(SWA_Q // IN_TILE))
    + ((SWA_Q + 2 * SWA_KV, SWA_Q, X_WIDTH), (SWA_Q, SWA_Q + X_WIDTH, IN_TILE)))
SWA_CHUNKS = (0, 1024, 2048, SWA_PROJ)
SWA_BLOCKS_PER_STEP = 2

D_FF = 4 * D_MODEL

LANES = 128
BF16_SUBLANES = 16
MIB = 1024 * 1024


def _params(semantics, vmem_mib):
    return pltpu.CompilerParams(dimension_semantics=semantics,
                                vmem_limit_bytes=vmem_mib * MIB)


def _rmsnorm_rows(x_ref, g_ref, dst_ref, chunk=256):
    g = g_ref[...]

    def body(i, carry):
        r = pl.ds(pl.multiple_of(i * chunk, chunk), chunk)
        x = x_ref[r, :]
        ms = jnp.mean(x * x, axis=-1, keepdims=True)
        dst_ref[r, :] = (x * lax.rsqrt(ms + RMS_EPS) * g).astype(dst_ref.dtype)
        return carry

    lax.fori_loop(0, x_ref.shape[0] // chunk, body, 0)


def _dot(a, b):
    return jnp.dot(a, b, preferred_element_type=F32)


def _dot_t0(a, b):
    return lax.dot_general(a, b, (((0,), (0,)), ((), ())), preferred_element_type=F32)


def _dot_t1(a, b):
    return lax.dot_general(a, b, (((1,), (1,)), ((), ())), preferred_element_type=F32)


def _split_bf16(x):
    hi = x.astype(BF16)
    lo = (x - hi.astype(F32)).astype(BF16)
    return hi, lo


def _round_robin(generators):
    pending = list(generators)
    while pending:
        pending = [gen for gen in pending if next(gen, True) is None]


def _row_tiles(x_hbm, xbuf, sem):
    i = pl.program_id(0)
    tm = xbuf.shape[0]

    def copy(idx):
        rows = pl.ds(pl.multiple_of(idx * tm, tm), tm)
        return pltpu.make_async_copy(x_hbm.at[rows, :], xbuf, sem)

    def start_first():
        @pl.when(i == 0)
        def _():
            copy(0).start()

    def wait():
        copy(i).wait()

    def request_next():
        @pl.when(i + 1 < pl.num_programs(0))
        def _():
            copy(i + 1).start()

    return start_first, wait, request_next


def _norm_matmul_kernel(x_ref, g_ref, w_ref, o_ref, xn_ref):
    @pl.when(pl.program_id(1) == 0)
    def _():
        _rmsnorm_rows(x_ref, g_ref, xn_ref)

    o_ref[...] = _dot(xn_ref[...], w_ref[...].astype(BF16)).astype(o_ref.dtype)


def _norm_matmul(x, g, w, tm):
    t, d = x.shape
    layers, _, n1 = w.shape
    return pl.pallas_call(
        _norm_matmul_kernel,
        out_shape=jax.ShapeDtypeStruct((t, layers * n1), BF16),
        grid=(t // tm, layers),
        in_specs=[
            pl.BlockSpec((tm, d), lambda i, j: (i, 0)),
            pl.BlockSpec((1, d), lambda i, j: (0, 0)),
            pl.BlockSpec((None, d, n1), lambda i, j: (j, 0, 0)),
        ],
        out_specs=pl.BlockSpec((tm, n1), lambda i, j: (i, j)),
        scratch_shapes=[pltpu.VMEM((tm, d), BF16)],
        compiler_params=_params(("parallel", "arbitrary"), 52),
        name="norm_matmul",
    )(x, g.reshape(1, d), w)


def _in_proj_kernel(x_hbm, g_ref, w_hbm, o_ref, xbuf, xn_ref, wres, wbuf, sems,
                    *, layer, pieces, transposed, chunks):
    start_first_rows, wait_rows, request_next_rows = _row_tiles(x_hbm, xbuf, sems.at[0])
    start_first_rows()

    def piece_copy(p, slot):
        src, _, width = pieces[p]
        if transposed:
            return pltpu.make_async_copy(w_hbm.at[layer, pl.ds(src, width), :],
                                         wbuf.at[slot, pl.ds(0, width), :], sems.at[1 + slot])
        return pltpu.make_async_copy(w_hbm.at[layer, :, pl.ds(src, width)],
                                     wbuf.at[slot, :, pl.ds(0, width)], sems.at[1 + slot])

    @pl.when(pl.program_id(0) == 0)
    def _():
        piece_copy(0, 0).start()
        for p, (_, dst, width) in enumerate(pieces):
            slot = p % 2
            if p + 1 < len(pieces):
                piece_copy(p + 1, 1 - slot).start()
            piece_copy(p, slot).wait()
            if transposed:
                out_w = -(-width // LANES) * LANES
                tile = wbuf[slot, 0:out_w, :]
                if width < out_w:
                    r = lax.broadcasted_iota(jnp.int32, tile.shape, 0)
                    tile = jnp.where(r < width, tile, 0.0)
                wres[:, dst:dst + out_w] = tile.T.astype(BF16)
            else:
                wres[:, dst:dst + width] = wbuf[slot, :, 0:width].astype(BF16)

    wait_rows()
    _rmsnorm_rows(xbuf, g_ref, xn_ref)
    request_next_rows()
    for n0, n1 in zip(chunks[:-1], chunks[1:]):
        o_ref[:, n0:n1] = _dot(xn_ref[...], wres[:, n0:n1]).astype(o_ref.dtype)


def _in_proj(x, g, w, layer, pieces, chunks, transposed, tm):
    t, d = x.shape
    n_out = chunks[-1]
    stage = (2, IN_TILE, d) if transposed else (2, d, IN_TILE)
    return pl.pallas_call(
        functools.partial(_in_proj_kernel, layer=layer, pieces=pieces, transposed=transposed,
                          chunks=chunks),
        out_shape=jax.ShapeDtypeStruct((t, n_out), BF16),
        grid=(t // tm,),
        in_specs=[
            pl.BlockSpec(memory_space=pl.ANY),
            pl.BlockSpec((1, d), lambda i: (0, 0)),
            pl.BlockSpec(memory_space=pl.ANY),
        ],
        out_specs=pl.BlockSpec((tm, n_out), lambda i: (i, 0)),
        scratch_shapes=[
            pltpu.VMEM((tm, d), F32),
            pltpu.VMEM((tm, d), BF16),
            pltpu.VMEM((d, n_out), BF16),
            pltpu.VMEM(stage, F32),
            pltpu.SemaphoreType.DMA((3,)),
        ],
        compiler_params=_params(("arbitrary",), 60),
        name="in_proj",
    )(x, g.reshape(1, d), w)


def _gla_kernel(q_ref, k_ref, v_ref, go_ref, glr_ref, wg_ref, bg_ref, gn_ref,
                o_ref, state_ref):
    @pl.when(pl.program_id(2) == 0)
    def _():
        state_ref[...] = jnp.zeros_like(state_ref)

    c, sub = GLA_CHUNK, GLA_SUB
    nc = sub // c
    shift = c.bit_length() - 1
    row = lax.broadcasted_iota(jnp.int32, (sub, sub), 0)
    col = lax.broadcasted_iota(jnp.int32, (sub, sub), 1)
    causal = (jnp.right_shift(row, shift) == jnp.right_shift(col, shift)) & (row >= col)
    tri = jnp.where(causal, 1.0, 0.0).astype(BF16)
    prow = lax.broadcasted_iota(jnp.int32, (BF16_SUBLANES, nc * LANES), 0)
    pcol = lax.broadcasted_iota(jnp.int32, (BF16_SUBLANES, nc * LANES), 1)
    sel = jnp.where((prow < 2 * nc) & ((prow & (nc - 1)) == pcol // LANES), 1.0, 0.0).astype(BF16)
    gn = gn_ref[...]
    heads = q_ref.shape[1] // GLA_DK

    def head_sub_block(hh, r):
        ks = slice(hh * GLA_DK, (hh + 1) * GLA_DK)
        vs = slice(hh * GLA_DV, (hh + 1) * GLA_DV)
        pre = _dot(glr_ref[r, :], wg_ref[:, ks]) + bg_ref[:, ks]
        yield
        log_sig = jnp.minimum(pre, 0.0) - jnp.log(1.0 + jnp.exp(-jnp.abs(pre)))
        log_a = log_sig * (1.0 / GLA_TAU)
        la_hi, la_lo = _split_bf16(log_a)
        b = _dot(tri, la_hi) + _dot(tri, la_lo)
        yield
        b_last = [b[(i + 1) * c - 1:(i + 1) * c, :] for i in range(nc)]
        bl_hi, bl_lo = _split_bf16(jnp.concatenate(b_last, axis=0))
        pad = jnp.zeros((BF16_SUBLANES - 2 * nc, GLA_DK), BF16)
        b_last_cols = _dot_t0(jnp.concatenate([bl_hi, bl_lo, pad], axis=0), sel)
        b_last_rows = jnp.concatenate(
            [jnp.broadcast_to(x, (c, GLA_DK)) for x in b_last], axis=0)

        q = q_ref[r, ks].astype(F32) * (GLA_DK ** -0.5)
        k = k_ref[r, ks].astype(F32)
        v = v_ref[r, vs]
        q_in = (q * jnp.exp(b)).astype(BF16)
        k_in = (k * jnp.exp(-b)).astype(BF16)
        a_raw = _dot_t1(q_in, k_in)
        k_out = (k * jnp.exp(b_last_rows - b)).astype(BF16)
        kv = [_dot_t0(k_out[i * c:(i + 1) * c], v[i * c:(i + 1) * c]) for i in range(nc)]
        yield
        a = jnp.where(causal, a_raw, 0.0).astype(BF16)
        o_intra = _dot(a, v)
        decays = []
        for i in range(nc):
            decay = jnp.exp(b_last_cols[:, i * LANES:(i + 1) * LANES])
            decays.append(jnp.concatenate([decay] * (GLA_DV // LANES), axis=1))
        yield
        state = state_ref[hh]
        outs = []
        for i in range(nc):
            cs = slice(i * c, (i + 1) * c)
            outs.append(o_intra[cs] + _dot(q_in[cs], state.astype(BF16)))
            state = decays[i] * state + kv[i]
        state_ref[hh] = state
        yield
        o = jnp.concatenate(outs, axis=0)
        ms = jnp.mean(o * o, axis=-1, keepdims=True)
        y = o * lax.rsqrt(ms + RMS_EPS) * gn
        go = go_ref[r, vs].astype(F32)
        o_ref[r, vs] = (y * (go * jax.nn.sigmoid(go))).astype(o_ref.dtype)

    def sub_block(si, carry):
        r = pl.ds(pl.multiple_of(si * sub, sub), sub)
        _round_robin(head_sub_block(hh, r) for hh in range(heads))
        return carry

    lax.fori_loop(0, q_ref.shape[0] // sub, sub_block, 0)


def _gla_mixer(proj, w_gate, b_gate, out_g, batch, seq, ts, hpb):
    t = proj.shape[0]
    ns = seq // ts
    kw, vw = hpb * GLA_DK, hpb * GLA_DV
    vb, gb = 0, GLA_V // vw
    qb, kb = 2 * GLA_V // kw, (2 * GLA_V + GLA_K) // kw
    rb = (2 * GLA_V + 2 * GLA_K + X_WIDTH) // GLA_RANK_PAD

    def rows(b, h, c):
        return b * ns + c

    return pl.pallas_call(
        _gla_kernel,
        out_shape=jax.ShapeDtypeStruct((t, GLA_V), BF16),
        grid=(batch, GLA_HEADS // hpb, ns),
        in_specs=[
            pl.BlockSpec((ts, kw), lambda b, h, c: (rows(b, h, c), qb + h)),
            pl.BlockSpec((ts, kw), lambda b, h, c: (rows(b, h, c), kb + h)),
            pl.BlockSpec((ts, vw), lambda b, h, c: (rows(b, h, c), vb + h)),
            pl.BlockSpec((ts, vw), lambda b, h, c: (rows(b, h, c), gb + h)),
            pl.BlockSpec((ts, GLA_RANK_PAD), lambda b, h, c: (rows(b, h, c), rb)),
            pl.BlockSpec((GLA_RANK_PAD, kw), lambda b, h, c: (0, h)),
            pl.BlockSpec((1, kw), lambda b, h, c: (0, h)),
            pl.BlockSpec((1, GLA_DV), lambda b, h, c: (0, 0)),
        ],
        out_specs=pl.BlockSpec((ts, vw), lambda b, h, c: (rows(b, h, c), h)),
        scratch_shapes=[pltpu.VMEM((hpb, GLA_DK, GLA_DV), F32)],
        compiler_params=_params(("parallel", "parallel", "arbitrary"), 40),
        name="gla_mixer",
    )(proj, proj, proj, proj, proj, w_gate, b_gate.reshape(1, GLA_K),
      out_g.reshape(1, GLA_DV))


def _swa_kernel(sinks_ref, q_ref, kvp_ref, kvc_ref, o_ref, bias_ref):
    n = pl.program_id(1)
    blk = SWA_BLOCK
    hd = SWA_HEAD_DIM
    n_sub = q_ref.shape[0] // blk

    @pl.when(n == 0)
    def _():
        qpos = lax.broadcasted_iota(jnp.int32, (blk, 2 * blk), 0) + blk
        kpos = lax.broadcasted_iota(jnp.int32, (blk, 2 * blk), 1)
        dist = qpos - kpos
        window = (dist >= 0) & (dist < SWA_WINDOW)
        distf = dist.astype(F32)
        for h in range(SWA_Q_HEADS):
            slope = 2.0 ** (-8.0 * (h + 1) / SWA_Q_HEADS)
            bias_ref[0, h] = jnp.where(window, -slope * distf, NEG_INF)
            bias_ref[1, h] = jnp.where(window & (kpos >= blk), -slope * distf, NEG_INF)

    ones = jnp.ones((2 * blk, hd), BF16)
    low_half = lax.broadcasted_iota(jnp.int32, (blk, 2 * hd), 1) < hd
    half = SWA_GROUP // 2
    first_table = jnp.where(n == 0, 1, 0)

    def kv_group(sb, kv):
        rows = slice(sb * blk, (sb + 1) * blk)
        prev = kvp_ref if sb == 0 else kvc_ref.at[(sb - 1) * blk:sb * blk, :]
        table = first_table if sb == 0 else 0
        ks = slice(kv * hd, (kv + 1) * hd)
        vs = slice(SWA_KV + kv * hd, SWA_KV + (kv + 1) * hd)
        kb = jnp.concatenate([prev[:, ks], kvc_ref[rows, ks]], axis=0)
        vb = jnp.concatenate([prev[:, vs], kvc_ref[rows, vs]], axis=0)
        v_even = jnp.concatenate([vb, ones], axis=1)
        v_odd = jnp.concatenate([ones, vb], axis=1)
        first = kv * SWA_GROUP
        heads = list(range(first, first + SWA_GROUP, 2)) + list(range(first + 1, first + SWA_GROUP, 2))
        qs = jnp.concatenate([q_ref[rows, h * hd:(h + 1) * hd] for h in heads], axis=0)
        s = _dot_t1(qs * (hd ** -0.5), kb)
        yield

        def softmax_half(lo):
            ps, es = [], []
            for g in range(lo, lo + half):
                h = heads[g]
                sg = s[g * blk:(g + 1) * blk] + bias_ref[table, h]
                m = jnp.maximum(jnp.max(sg, axis=-1, keepdims=True), sinks_ref[h])
                ps.append(jnp.exp(sg - m).astype(BF16))
                es.append(jnp.exp(sinks_ref[h] - m))
            return jnp.concatenate(ps, axis=0), es

        p_even, e_even = softmax_half(0)
        pv_even = _dot(p_even, v_even)
        yield
        p_odd, e_odd = softmax_half(half)
        pv_odd = _dot(p_odd, v_odd)
        yield
        for t in range(half):
            prows = slice(t * blk, (t + 1) * blk)
            pe, po = pv_even[prows], pv_odd[prows]
            num = jnp.where(low_half, pe, po)
            den = pltpu.roll(jnp.where(low_half, po, pe), hd, axis=1)
            den = den + jnp.where(low_half, e_even[t], e_odd[t])
            pair = slice((first + 2 * t) * hd, (first + 2 * t + 2) * hd)
            o_ref[rows, pair] = (num / den).astype(o_ref.dtype)

    for sb in range(n_sub):
        groups = [kv_group(sb, kv) for kv in range(SWA_KV_HEADS)]
        order = [0, 1, 0, 0]
        for g in range(1, len(groups)):
            order += ([g + 1] if g + 1 < len(groups) else []) + [g, g - 1, g]
        order.append(len(groups) - 1)
        for g in order:
            next(groups[g], None)


def _swa_mixer(proj, sinks, batch, seq, blocks_per_step):
    t = proj.shape[0]
    nb = seq // SWA_BLOCK
    bps = blocks_per_step
    ns = nb // bps
    kvb = (SWA_Q + X_WIDTH) // IN_TILE
    return pl.pallas_call(
        _swa_kernel,
        out_shape=jax.ShapeDtypeStruct((t, SWA_Q), BF16),
        grid=(batch, ns),
        in_specs=[
            pl.BlockSpec(memory_space=pltpu.SMEM),
            pl.BlockSpec((bps * SWA_BLOCK, SWA_Q), lambda b, n: (b * ns + n, 0)),
            pl.BlockSpec((SWA_BLOCK, IN_TILE),
                         lambda b, n: (b * nb + jnp.maximum(bps * n - 1, 0), kvb)),
            pl.BlockSpec((bps * SWA_BLOCK, IN_TILE), lambda b, n: (b * ns + n, kvb)),
        ],
        out_specs=pl.BlockSpec((bps * SWA_BLOCK, SWA_Q), lambda b, n: (b * ns + n, 0)),
        scratch_shapes=[pltpu.VMEM((2, SWA_Q_HEADS, SWA_BLOCK, 2 * SWA_BLOCK), F32)],
        compiler_params=_params(("parallel", "arbitrary"), 40),
        name="swa_mixer",
    )(sinks, proj, proj, proj)


def _memattn_kernel(xq_ref, mk_ref, mv_ref, o_ref):
    hd = X_HEAD_DIM
    ones = jnp.ones((MEM_LEN, hd), BF16)

    def head(h):
        hs = slice(h * hd, (h + 1) * hd)
        s = _dot_t1(xq_ref[:, hs], mk_ref[:, hs]) * (hd ** -0.5)
        yield
        m = jnp.max(s, axis=-1, keepdims=True)
        p = jnp.exp(s - m).astype(BF16)
        pv = _dot(p, jnp.concatenate([mv_ref[:, hs], ones], axis=1))
        yield
        o_ref[:, hs] = (pv[:, :hd] / pv[:, hd:]).astype(o_ref.dtype)

    _round_robin(head(h) for h in range(X_HEADS))


def _mem_attention(proj, xq_block, mkv, layer, batch, seq, tm):
    t = proj.shape[0]
    ns = seq // tm
    return pl.pallas_call(
        _memattn_kernel,
        out_shape=jax.ShapeDtypeStruct((t, X_WIDTH), BF16),
        grid=(batch, ns),
        in_specs=[
            pl.BlockSpec((tm, X_WIDTH), lambda b, s: (b * ns + s, xq_block)),
            pl.BlockSpec((MEM_LEN, X_WIDTH), lambda b, s: (b, 2 * layer)),
            pl.BlockSpec((MEM_LEN, X_WIDTH), lambda b, s: (b, 2 * layer + 1)),
        ],
        out_specs=pl.BlockSpec((tm, X_WIDTH), lambda b, s: (b * ns + s, 0)),
        compiler_params=_params(("parallel", "arbitrary"), 32),
        name="mem_attention",
    )(proj, mkv, mkv)


def _out_proj_kernel(a_ref, b_ref, wa_ref, wb_ref, x_ref, o_ref):
    acc = _dot(a_ref[...], wa_ref[...].astype(BF16)) + _dot(b_ref[...], wb_ref[...].astype(BF16))
    o_ref[...] = x_ref[...] + acc


def _out_proj(o_mix, o_mem, w_out, layer, x, tm):
    t, d = x.shape
    ka, kb = o_mix.shape[1], o_mem.shape[1]
    once = pl.Buffered(1)
    return pl.pallas_call(
        _out_proj_kernel,
        out_shape=jax.ShapeDtypeStruct((t, d), F32),
        grid=(t // tm,),
        in_specs=[
            pl.BlockSpec((tm, ka), lambda i: (i, 0)),
            pl.BlockSpec((tm, kb), lambda i: (i, 0)),
            pl.BlockSpec((None, ka, d), lambda i: (layer, 0, 0), pipeline_mode=once),
            pl.BlockSpec((None, kb, d), lambda i: (layer, ka // kb, 0), pipeline_mode=once),
            pl.BlockSpec((tm, d), lambda i: (i, 0)),
        ],
        out_specs=pl.BlockSpec((tm, d), lambda i: (i, 0)),
        compiler_params=_params(("parallel",), 56),
        name="out_proj",
    )(o_mix, o_mem, w_out, w_out, x)


def _mlp_kernel(x_hbm, g_ref, wu_hbm, wd_hbm, gf_ref, o_ref, xbuf, xn_ref, wu_buf, wd_buf,
                sems, xsem, *, layer, final_norm):
    tf = wu_buf.shape[2]
    n_tiles = wu_hbm.shape[2] // tf

    def tile_copies(j, slot):
        off = pl.multiple_of(j * tf, tf)
        return (
            pltpu.make_async_copy(wu_hbm.at[layer, :, pl.ds(off, tf)], wu_buf.at[slot],
                                  sems.at[0, slot]),
            pltpu.make_async_copy(wd_hbm.at[layer, pl.ds(off, tf), :], wd_buf.at[slot],
                                  sems.at[1, slot]),
        )

    i = pl.program_id(0)
    start_first_rows, wait_rows, request_next_rows = _row_tiles(x_hbm, xbuf, xsem.at[0])
    start_first_rows()

    @pl.when(i == 0)
    def _():
        for cp in tile_copies(0, 0):
            cp.start()

    wait_rows()
    _rmsnorm_rows(xbuf, g_ref, xn_ref)
    o_ref[...] = xbuf[...]
    request_next_rows()

    def tile_pair(jj, carry):
        for slot in (0, 1):
            j = 2 * jj + slot
            nxt = lax.rem(j + 1, n_tiles)

            @pl.when(jnp.logical_or(j + 1 < n_tiles, i + 1 < pl.num_programs(0)))
            def _():
                for cp in tile_copies(nxt, 1 - slot):
                    cp.start()

            for cp in tile_copies(j, slot):
                cp.wait()
            h = _dot(xn_ref[...], wu_buf[slot].astype(BF16))
            h = jnp.square(jnp.maximum(h, 0.0)).astype(BF16)
            o_ref[...] += _dot(h, wd_buf[slot].astype(BF16))
        return carry

    lax.fori_loop(0, n_tiles // 2, tile_pair, 0)

    if final_norm:
        _rmsnorm_rows(o_ref, gf_ref, o_ref)


def _mlp(x, g, w_up, w_down, layer, g_final, final_norm, tm, tf):
    t, d = x.shape
    f = w_up.shape[2]
    assert f % (2 * tf) == 0
    return pl.pallas_call(
        functools.partial(_mlp_kernel, layer=layer, final_norm=final_norm),
        out_shape=jax.ShapeDtypeStruct((t, d), F32),
        grid=(t // tm,),
        in_specs=[
            pl.BlockSpec(memory_space=pl.ANY),
            pl.BlockSpec((1, d), lambda i: (0, 0)),
            pl.BlockSpec(memory_space=pl.ANY),
            pl.BlockSpec(memory_space=pl.ANY),
            pl.BlockSpec((1, d), lambda i: (0, 0)),
        ],
        out_specs=pl.BlockSpec((tm, d), lambda i: (i, 0)),
        scratch_shapes=[
            pltpu.VMEM((tm, d), F32),
            pltpu.VMEM((tm, d), BF16),
            pltpu.VMEM((2, d, tf), F32),
            pltpu.VMEM((2, tf, d), F32),
            pltpu.SemaphoreType.DMA((2, 2)),
            pltpu.SemaphoreType.DMA((1,)),
        ],
        compiler_params=_params(("arbitrary",), 58),
        name="mlp",
    )(x, g.reshape(1, d), w_up, w_down, g_final.reshape(1, d))


def kernel(x, mem, mem_norm_g, attn_norm_g, w_in_gla, w_gate_up, b_gate, gla_out_norm_g,
           w_in_swa, sinks, w_mem_kv, w_out, mlp_norm_g, w_up, w_down, final_norm_g):
    batch, seq, d = x.shape
    t = batch * seq
    xf = x.reshape(t, d)
    assert X_WIDTH == IN_TILE and SWA_KV_HEADS == 3

    mkv = _norm_matmul(mem.reshape(batch * MEM_LEN, d), mem_norm_g, w_mem_kv,
                       tm=batch * MEM_LEN)

    w_in_gla_t = jnp.swapaxes(w_in_gla, 1, 2)

    for i in range(DEPTH):
        j = i // 2
        if i % 2 == 0:
            proj = _in_proj(xf, attn_norm_g[i], w_in_gla_t, j, GLA_PIECES, GLA_CHUNKS,
                            transposed=True, tm=512)
            w_gate = jnp.pad(w_gate_up[j], ((0, GLA_RANK_PAD - GLA_GATE_RANK), (0, 0))).astype(BF16)
            o_mix = _gla_mixer(proj, w_gate, b_gate[j], gla_out_norm_g[j], batch, seq,
                               ts=1024, hpb=4)
            xq_block = (2 * GLA_V + 2 * GLA_K) // X_WIDTH
        else:
            proj = _in_proj(xf, attn_norm_g[i], w_in_swa, j, SWA_PIECES, SWA_CHUNKS,
                            transposed=False, tm=512)
            o_mix = _swa_mixer(proj, sinks[j], batch, seq, SWA_BLOCKS_PER_STEP)
            xq_block = SWA_Q // X_WIDTH
        o_mem = _mem_attention(proj, xq_block, mkv, i, batch, seq, tm=1024)
        xf = _out_proj(o_mix, o_mem, w_out, i, xf, tm=512)
        xf = _mlp(xf, mlp_norm_g[i], w_up, w_down, i, final_norm_g,
                  final_norm=(i == DEPTH - 1), tm=1024, tf=512)
    return xf.reshape(batch, seq, d)
```

```python
import functools

import jax
import jax.numpy as jnp
from jax import lax
from jax.experimental import pallas as pl
from jax.experimental.pallas import tpu as pltpu

F32 = jnp.float32
BF16 = jnp.bfloat16

D_MODEL = 2048
DEPTH = 4
MEM_LEN = 256
RMS_EPS = 1e-5
NEG_INF = -1e30

X_HEADS = 4
X_HEAD_DIM = 128
X_WIDTH = X_HEADS * X_HEAD_DIM

GLA_HEADS = 4
GLA_DK = 256
GLA_DV = 384
GLA_K = GLA_HEADS * GLA_DK
GLA_V = GLA_HEADS * GLA_DV
GLA_GATE_RANK = 16
GLA_TAU = 16.0
GLA_CHUNK = 64
GLA_SUB = 256
GLA_SUBS_PER_ITER = 1
GLA_RANK_PAD = 128
IN_TILE = 512
GLA_PROJ = 2 * GLA_V + 2 * GLA_K + X_WIDTH + GLA_RANK_PAD
GLA_PIECES = (
    tuple((2 * GLA_K + i * IN_TILE, i * IN_TILE, IN_TILE) for i in range(2 * GLA_V // IN_TILE))
    + tuple((i * IN_TILE, 2 * GLA_V + i * IN_TILE, IN_TILE) for i in range(2 * GLA_K // IN_TILE))
    + ((2 * GLA_K + 2 * GLA_V + GLA_GATE_RANK, 2 * GLA_K + 2 * GLA_V, X_WIDTH),
       (2 * GLA_K + 2 * GLA_V, 2 * GLA_K + 2 * GLA_V + X_WIDTH, GLA_GATE_RANK)))
GLA_CHUNKS = (0, 2048, 4096, GLA_PROJ)

SWA_HEAD_DIM = 64
SWA_Q_HEADS = 24
SWA_GROUP = 8
SWA_KV_HEADS = 3
SWA_WINDOW = 128
SWA_BLOCK = 128
SWA_Q = SWA_Q_HEADS * SWA_HEAD_DIM
SWA_KV = SWA_KV_HEADS * SWA_HEAD_DIM
SWA_PROJ = SWA_Q + X_WIDTH + IN_TILE
SWA_PIECES = (
    tuple((i * IN_TILE, i * IN_TILE, IN_TILE) for i in range(SWA_Q // IN_TILE))
    + ((SWA_Q + 2 * SWA_KV, SWA_Q, X_WIDTH), (SWA_Q, SWA_Q + X_WIDTH, IN_TILE)))
SWA_CHUNKS = (0, 1024, 2048, SWA_PROJ)
SWA_BLOCKS_PER_STEP = 2

D_FF = 4 * D_MODEL

LANES = 128
BF16_SUBLANES = 16
MIB = 1024 * 1024


def _params(semantics, vmem_mib):
    return pltpu.CompilerParams(dimension_semantics=semantics,
                                vmem_limit_bytes=vmem_mib * MIB)


def _rmsnorm_rows(x_ref, g_ref, dst_ref, chunk=256):
    g = g_ref[...]

    def body(i, carry):
        r = pl.ds(pl.multiple_of(i * chunk, chunk), chunk)
        x = x_ref[r, :]
        ms = jnp.mean(x * x, axis=-1, keepdims=True)
        dst_ref[r, :] = (x * lax.rsqrt(ms + RMS_EPS) * g).astype(dst_ref.dtype)
        return carry

    lax.fori_loop(0, x_ref.shape[0] // chunk, body, 0)


def _dot(a, b):
    return jnp.dot(a, b, preferred_element_type=F32)


def _dot_t0(a, b):
    return lax.dot_general(a, b, (((0,), (0,)), ((), ())), preferred_element_type=F32)


def _dot_t1(a, b):
    return lax.dot_general(a, b, (((1,), (1,)), ((), ())), preferred_element_type=F32)


def _split_bf16(x):
    hi = x.astype(BF16)
    lo = (x - hi.astype(F32)).astype(BF16)
    return hi, lo


def _round_robin(generators):
    pending = list(generators)
    while pending:
        pending = [gen for gen in pending if next(gen, True) is None]


def _row_tiles(x_hbm, xbuf, sem):
    i = pl.program_id(0)
    tm = xbuf.shape[0]

    def copy(idx):
        rows = pl.ds(pl.multiple_of(idx * tm, tm), tm)
        return pltpu.make_async_copy(x_hbm.at[rows, :], xbuf, sem)

    def start_first():
        @pl.when(i == 0)
        def _():
            copy(0).start()

    def wait():
        copy(i).wait()

    def request_next():
        @pl.when(i + 1 < pl.num_programs(0))
        def _():
            copy(i + 1).start()

    return start_first, wait, request_next


def _norm_matmul_kernel(x_ref, g_ref, w_ref, o_ref, xn_ref):
    @pl.when(pl.program_id(1) == 0)
    def _():
        _rmsnorm_rows(x_ref, g_ref, xn_ref)

    o_ref[...] = _dot(xn_ref[...], w_ref[...].astype(BF16)).astype(o_ref.dtype)


def _norm_matmul(x, g, w, tm):
    t, d = x.shape
    layers, _, n1 = w.shape
    return pl.pallas_call(
        _norm_matmul_kernel,
        out_shape=jax.ShapeDtypeStruct((t, layers * n1), BF16),
        grid=(t // tm, layers),
        in_specs=[
            pl.BlockSpec((tm, d), lambda i, j: (i, 0)),
            pl.BlockSpec((1, d), lambda i, j: (0, 0)),
            pl.BlockSpec((None, d, n1), lambda i, j: (j, 0, 0)),
        ],
        out_specs=pl.BlockSpec((tm, n1), lambda i, j: (i, j)),
        scratch_shapes=[pltpu.VMEM((tm, d), BF16)],
        compiler_params=_params(("parallel", "arbitrary"), 52),
        name="norm_matmul",
    )(x, g.reshape(1, d), w)


def _in_proj_kernel(x_hbm, g_ref, w_hbm, o_ref, xbuf, xn_ref, wres, wbuf, sems,
                    *, layer, pieces, transposed, chunks):
    start_first_rows, wait_rows, request_next_rows = _row_tiles(x_hbm, xbuf, sems.at[0])
    start_first_rows()

    def piece_copy(p, slot):
        src, _, width = pieces[p]
        if transposed:
            return pltpu.make_async_copy(w_hbm.at[layer, pl.ds(src, width), :],
                                         wbuf.at[slot, pl.ds(0, width), :], sems.at[1 + slot])
        return pltpu.make_async_copy(w_hbm.at[layer, :, pl.ds(src, width)],
                                     wbuf.at[slot, :, pl.ds(0, width)], sems.at[1 + slot])

    @pl.when(pl.program_id(0) == 0)
    def _():
        piece_copy(0, 0).start()
        for p, (_, dst, width) in enumerate(pieces):
            slot = p % 2
            if p + 1 < len(pieces):
                piece_copy(p + 1, 1 - slot).start()
            piece_copy(p, slot).wait()
            if transposed:
                out_w = -(-width // LANES) * LANES
                tile = wbuf[slot, 0:out_w, :]
                if width < out_w:
                    r = lax.broadcasted_iota(jnp.int32, tile.shape, 0)
                    tile = jnp.where(r < width, tile, 0.0)
                wres[:, dst:dst + out_w] = tile.T.astype(BF16)
            else:
                wres[:, dst:dst + width] = wbuf[slot, :, 0:width].astype(BF16)

    wait_rows()
    _rmsnorm_rows(xbuf, g_ref, xn_ref)
    request_next_rows()
    for n0, n1 in zip(chunks[:-1], chunks[1:]):
        o_ref[:, n0:n1] = _dot(xn_ref[...], wres[:, n0:n1]).astype(o_ref.dtype)


def _in_proj(x, g, w, layer, pieces, chunks, transposed, tm):
    t, d = x.shape
    n_out = chunks[-1]
    stage = (2, IN_TILE, d) if transposed else (2, d, IN_TILE)
    return pl.pallas_call(
        functools.partial(_in_proj_kernel, layer=layer, pieces=pieces, transposed=transposed,
                          chunks=chunks),
        out_shape=jax.ShapeDtypeStruct((t, n_out), BF16),
        grid=(t // tm,),
        in_specs=[
            pl.BlockSpec(memory_space=pl.ANY),
            pl.BlockSpec((1, d), lambda i: (0, 0)),
            pl.BlockSpec(memory_space=pl.ANY),
        ],
        out_specs=pl.BlockSpec((tm, n_out), lambda i: (i, 0)),
        scratch_shapes=[
            pltpu.VMEM((tm, d), F32),
            pltpu.VMEM((tm, d), BF16),
            pltpu.VMEM((d, n_out), BF16),
            pltpu.VMEM(stage, F32),
            pltpu.SemaphoreType.DMA((3,)),
        ],
        compiler_params=_params(("arbitrary",), 60),
        name="in_proj",
    )(x, g.reshape(1, d), w)


def _gla_kernel(q_ref, k_ref, v_ref, go_ref, glr_ref, wg_ref, bg_ref, gn_ref,
                o_ref, state_ref):
    @pl.when(pl.program_id(2) == 0)
    def _():
        state_ref[...] = jnp.zeros_like(state_ref)

    c, sub = GLA_CHUNK, GLA_SUB
    nc = sub // c
    shift = c.bit_length() - 1
    row = lax.broadcasted_iota(jnp.int32, (sub, sub), 0)
    col = lax.broadcasted_iota(jnp.int32, (sub, sub), 1)
    causal = (jnp.right_shift(row, shift) == jnp.right_shift(col, shift)) & (row >= col)
    tri = jnp.where(causal, 1.0, 0.0).astype(BF16)
    prow = lax.broadcasted_iota(jnp.int32, (BF16_SUBLANES, nc * LANES), 0)
    pcol = lax.broadcasted_iota(jnp.int32, (BF16_SUBLANES, nc * LANES), 1)
    sel = jnp.where((prow < 2 * nc) & ((prow & (nc - 1)) == pcol // LANES), 1.0, 0.0).astype(BF16)
    gn = gn_ref[...]
    heads = q_ref.shape[1] // GLA_DK

    def head_sub_block(hh, r):
        ks = slice(hh * GLA_DK, (hh + 1) * GLA_DK)
        vs = slice(hh * GLA_DV, (hh + 1) * GLA_DV)
        pre = _dot(glr_ref[r, :], wg_ref[:, ks]) + bg_ref[:, ks]
        yield
        log_sig = jnp.minimum(pre, 0.0) - jnp.log(1.0 + jnp.exp(-jnp.abs(pre)))
        log_a = log_sig * (1.0 / GLA_TAU)
        la_hi, la_lo = _split_bf16(log_a)
        b = _dot(tri, la_hi) + _dot(tri, la_lo)
        yield
        b_last = [b[(i + 1) * c - 1:(i + 1) * c, :] for i in range(nc)]
        bl_hi, bl_lo = _split_bf16(jnp.concatenate(b_last, axis=0))
        pad = jnp.zeros((BF16_SUBLANES - 2 * nc, GLA_DK), BF16)
        b_last_cols = _dot_t0(jnp.concatenate([bl_hi, bl_lo, pad], axis=0), sel)
        b_last_rows = jnp.concatenate(
            [jnp.broadcast_to(x, (c, GLA_DK)) for x in b_last], axis=0)

        q = q_ref[r, ks].astype(F32) * (GLA_DK ** -0.5)
        k = k_ref[r, ks].astype(F32)
        v = v_ref[r, vs]
        q_in = (q * jnp.exp(b)).astype(BF16)
        k_in = (k * jnp.exp(-b)).astype(BF16)
        a_raw = _dot_t1(q_in, k_in)
        k_out = (k * jnp.exp(b_last_rows - b)).astype(BF16)
        kv = [_dot_t0(k_out[i * c:(i + 1) * c], v[i * c:(i + 1) * c]) for i in range(nc)]
        yield
        a = jnp.where(causal, a_raw, 0.0).astype(BF16)
        o_intra = _dot(a, v)
        decays = []
        for i in range(nc):
            decay = jnp.exp(b_last_cols[:, i * LANES:(i + 1) * LANES])
            decays.append(jnp.concatenate([decay] * (GLA_DV // LANES), axis=1))
        yield
        state = state_ref[hh]
        outs = []
        for i in range(nc):
            cs = slice(i * c, (i + 1) * c)
            outs.append(o_intra[cs] + _dot(q_in[cs], state.astype(BF16)))
            state = decays[i] * state + kv[i]
        state_ref[hh] = state
        yield
        o = jnp.concatenate(outs, axis=0)
        ms = jnp.mean(o * o, axis=-1, keepdims=True)
        y = o * lax.rsqrt(ms + RMS_EPS) * gn
        go = go_ref[r, vs].astype(F32)
        o_ref[r, vs] = (y * (go * jax.nn.sigmoid(go))).astype(o_ref.dtype)

    unroll = GLA_SUBS_PER_ITER

    def sub_blocks(si, carry):
        rows = [pl.ds(pl.multiple_of((si * unroll + u) * sub, sub), sub) for u in range(unroll)]
        _round_robin(head_sub_block(hh, r) for r in rows for hh in range(heads))
        return carry

    lax.fori_loop(0, q_ref.shape[0] // (sub * unroll), sub_blocks, 0)


def _gla_mixer(proj, w_gate, b_gate, out_g, batch, seq, ts, hpb):
    t = proj.shape[0]
    ns = seq // ts
    kw, vw = hpb * GLA_DK, hpb * GLA_DV
    vb, gb = 0, GLA_V // vw
    qb, kb = 2 * GLA_V // kw, (2 * GLA_V + GLA_K) // kw
    rb = (2 * GLA_V + 2 * GLA_K + X_WIDTH) // GLA_RANK_PAD

    def rows(b, h, c):
        return b * ns + c

    return pl.pallas_call(
        _gla_kernel,
        out_shape=jax.ShapeDtypeStruct((t, GLA_V), BF16),
        grid=(batch, GLA_HEADS // hpb, ns),
        in_specs=[
            pl.BlockSpec((ts, kw), lambda b, h, c: (rows(b, h, c), qb + h)),
            pl.BlockSpec((ts, kw), lambda b, h, c: (rows(b, h, c), kb + h)),
            pl.BlockSpec((ts, vw), lambda b, h, c: (rows(b, h, c), vb + h)),
            pl.BlockSpec((ts, vw), lambda b, h, c: (rows(b, h, c), gb + h)),
            pl.BlockSpec((ts, GLA_RANK_PAD), lambda b, h, c: (rows(b, h, c), rb)),
            pl.BlockSpec((GLA_RANK_PAD, kw), lambda b, h, c: (0, h)),
            pl.BlockSpec((1, kw), lambda b, h, c: (0, h)),
            pl.BlockSpec((1, GLA_DV), lambda b, h, c: (0, 0)),
        ],
        out_specs=pl.BlockSpec((ts, vw), lambda b, h, c: (rows(b, h, c), h)),
        scratch_shapes=[pltpu.VMEM((hpb, GLA_DK, GLA_DV), F32)],
        compiler_params=_params(("parallel", "parallel", "arbitrary"), 40),
        name="gla_mixer",
    )(proj, proj, proj, proj, proj, w_gate, b_gate.reshape(1, GLA_K),
      out_g.reshape(1, GLA_DV))


def _swa_kernel(sinks_ref, q_ref, kvp_ref, kvc_ref, o_ref, bias_ref):
    n = pl.program_id(1)
    blk = SWA_BLOCK
    hd = SWA_HEAD_DIM
    n_sub = q_ref.shape[0] // blk

    @pl.when(n == 0)
    def _():
        qpos = lax.broadcasted_iota(jnp.int32, (blk, 2 * blk), 0) + blk
        kpos = lax.broadcasted_iota(jnp.int32, (blk, 2 * blk), 1)
        dist = qpos - kpos
        window = (dist >= 0) & (dist < SWA_WINDOW)
        distf = dist.astype(F32)
        for h in range(SWA_Q_HEADS):
            slope = 2.0 ** (-8.0 * (h + 1) / SWA_Q_HEADS)
            bias_ref[0, h] = jnp.where(window, -slope * distf, NEG_INF)
            bias_ref[1, h] = jnp.where(window & (kpos >= blk), -slope * distf, NEG_INF)

    ones = jnp.ones((2 * blk, hd), BF16)
    low_half = lax.broadcasted_iota(jnp.int32, (blk, 2 * hd), 1) < hd
    half = SWA_GROUP // 2
    first_table = jnp.where(n == 0, 1, 0)

    def kv_group(sb, kv):
        rows = slice(sb * blk, (sb + 1) * blk)
        prev = kvp_ref if sb == 0 else kvc_ref.at[(sb - 1) * blk:sb * blk, :]
        table = first_table if sb == 0 else 0
        ks = slice(kv * hd, (kv + 1) * hd)
        vs = slice(SWA_KV + kv * hd, SWA_KV + (kv + 1) * hd)
        kb = jnp.concatenate([prev[:, ks], kvc_ref[rows, ks]], axis=0)
        vb = jnp.concatenate([prev[:, vs], kvc_ref[rows, vs]], axis=0)
        v_even = jnp.concatenate([vb, ones], axis=1)
        v_odd = jnp.concatenate([ones, vb], axis=1)
        first = kv * SWA_GROUP
        heads = list(range(first, first + SWA_GROUP, 2)) + list(range(first + 1, first + SWA_GROUP, 2))
        qs = jnp.concatenate([q_ref[rows, h * hd:(h + 1) * hd] for h in heads], axis=0)
        s = _dot_t1(qs * (hd ** -0.5), kb)
        yield

        def softmax_half(lo):
            ps, es = [], []
            for g in range(lo, lo + half):
                h = heads[g]
                sg = s[g * blk:(g + 1) * blk] + bias_ref[table, h]
                m = jnp.maximum(jnp.max(sg, axis=-1, keepdims=True), sinks_ref[h])
                ps.append(jnp.exp(sg - m).astype(BF16))
                es.append(jnp.exp(sinks_ref[h] - m))
            return jnp.concatenate(ps, axis=0), es

        p_even, e_even = softmax_half(0)
        pv_even = _dot(p_even, v_even)
        yield
        p_odd, e_odd = softmax_half(half)
        pv_odd = _dot(p_odd, v_odd)
        yield
        for t in range(half):
            prows = slice(t * blk, (t + 1) * blk)
            pe, po = pv_even[prows], pv_odd[prows]
            num = jnp.where(low_half, pe, po)
            den = pltpu.roll(jnp.where(low_half, po, pe), hd, axis=1)
            den = den + jnp.where(low_half, e_even[t], e_odd[t])
            pair = slice((first + 2 * t) * hd, (first + 2 * t + 2) * hd)
            o_ref[rows, pair] = (num / den).astype(o_ref.dtype)

    for sb in range(n_sub):
        groups = [kv_group(sb, kv) for kv in range(SWA_KV_HEADS)]
        order = [0, 1, 0, 0]
        for g in range(1, len(groups)):
            order += ([g + 1] if g + 1 < len(groups) else []) + [g, g - 1, g]
        order.append(len(groups) - 1)
        for g in order:
            next(groups[g], None)


def _swa_mixer(proj, sinks, batch, seq, blocks_per_step):
    t = proj.shape[0]
    nb = seq // SWA_BLOCK
    bps = blocks_per_step
    ns = nb // bps
    kvb = (SWA_Q + X_WIDTH) // IN_TILE
    return pl.pallas_call(
        _swa_kernel,
        out_shape=jax.ShapeDtypeStruct((t, SWA_Q), BF16),
        grid=(batch, ns),
        in_specs=[
            pl.BlockSpec(memory_space=pltpu.SMEM),
            pl.BlockSpec((bps * SWA_BLOCK, SWA_Q), lambda b, n: (b * ns + n, 0)),
            pl.BlockSpec((SWA_BLOCK, IN_TILE),
                         lambda b, n: (b * nb + jnp.maximum(bps * n - 1, 0), kvb)),
            pl.BlockSpec((bps * SWA_BLOCK, IN_TILE), lambda b, n: (b * ns + n, kvb)),
        ],
        out_specs=pl.BlockSpec((bps * SWA_BLOCK, SWA_Q), lambda b, n: (b * ns + n, 0)),
        scratch_shapes=[pltpu.VMEM((2, SWA_Q_HEADS, SWA_BLOCK, 2 * SWA_BLOCK), F32)],
        compiler_params=_params(("parallel", "arbitrary"), 40),
        name="swa_mixer",
    )(sinks, proj, proj, proj)


def _memattn_kernel(xq_ref, mk_ref, mv_ref, o_ref):
    hd = X_HEAD_DIM
    ones = jnp.ones((MEM_LEN, hd), BF16)

    def head(h):
        hs = slice(h * hd, (h + 1) * hd)
        s = _dot_t1(xq_ref[:, hs], mk_ref[:, hs]) * (hd ** -0.5)
        yield
        m = jnp.max(s, axis=-1, keepdims=True)
        p = jnp.exp(s - m).astype(BF16)
        pv = _dot(p, jnp.concatenate([mv_ref[:, hs], ones], axis=1))
        yield
        o_ref[:, hs] = (pv[:, :hd] / pv[:, hd:]).astype(o_ref.dtype)

    _round_robin(head(h) for h in range(X_HEADS))


def _mem_attention(proj, xq_block, mkv, layer, batch, seq, tm):
    t = proj.shape[0]
    ns = seq // tm
    return pl.pallas_call(
        _memattn_kernel,
        out_shape=jax.ShapeDtypeStruct((t, X_WIDTH), BF16),
        grid=(batch, ns),
        in_specs=[
            pl.BlockSpec((tm, X_WIDTH), lambda b, s: (b * ns + s, xq_block)),
            pl.BlockSpec((MEM_LEN, X_WIDTH), lambda b, s: (b, 2 * layer)),
            pl.BlockSpec((MEM_LEN, X_WIDTH), lambda b, s: (b, 2 * layer + 1)),
        ],
        out_specs=pl.BlockSpec((tm, X_WIDTH), lambda b, s: (b * ns + s, 0)),
        compiler_params=_params(("parallel", "arbitrary"), 32),
        name="mem_attention",
    )(proj, mkv, mkv)


def _out_proj_kernel(a_ref, b_ref, wa_ref, wb_ref, x_ref, o_ref):
    acc = _dot(a_ref[...], wa_ref[...].astype(BF16)) + _dot(b_ref[...], wb_ref[...].astype(BF16))
    o_ref[...] = x_ref[...] + acc


def _out_proj(o_mix, o_mem, w_out, layer, x, tm):
    t, d = x.shape
    ka, kb = o_mix.shape[1], o_mem.shape[1]
    once = pl.Buffered(1)
    return pl.pallas_call(
        _out_proj_kernel,
        out_shape=jax.ShapeDtypeStruct((t, d), F32),
        grid=(t // tm,),
        in_specs=[
            pl.BlockSpec((tm, ka), lambda i: (i, 0)),
            pl.BlockSpec((tm, kb), lambda i: (i, 0)),
            pl.BlockSpec((None, ka, d), lambda i: (layer, 0, 0), pipeline_mode=once),
            pl.BlockSpec((None, kb, d), lambda i: (layer, ka // kb, 0), pipeline_mode=once),
            pl.BlockSpec((tm, d), lambda i: (i, 0)),
        ],
        out_specs=pl.BlockSpec((tm, d), lambda i: (i, 0)),
        compiler_params=_params(("parallel",), 56),
        name="out_proj",
    )(o_mix, o_mem, w_out, w_out, x)


def _mlp_kernel(x_hbm, g_ref, wu_hbm, wd_hbm, gf_ref, o_ref, xbuf, xn_ref, xn_next, wu_buf,
                wd_buf, sems, xsem, *, layer, final_norm):
    tf = wu_buf.shape[2]
    n_tiles = wu_hbm.shape[2] // tf

    def tile_copies(j, slot):
        off = pl.multiple_of(j * tf, tf)
        return (
            pltpu.make_async_copy(wu_hbm.at[layer, :, pl.ds(off, tf)], wu_buf.at[slot],
                                  sems.at[0, slot]),
            pltpu.make_async_copy(wd_hbm.at[layer, pl.ds(off, tf), :], wd_buf.at[slot],
                                  sems.at[1, slot]),
        )

    i = pl.program_id(0)
    has_next = i + 1 < pl.num_programs(0)
    tm = xbuf.shape[0]
    piece = tm // (n_tiles // 2)
    g = g_ref[...]

    def x_copy(idx):
        rows = pl.ds(pl.multiple_of(idx * tm, tm), tm)
        return pltpu.make_async_copy(x_hbm.at[rows, :], xbuf, xsem.at[0])

    @pl.when(i == 0)
    def _():
        x_copy(0).start()
        for cp in tile_copies(0, 0):
            cp.start()
        x_copy(0).wait()
        _rmsnorm_rows(xbuf, g_ref, xn_ref)

    @pl.when(i > 0)
    def _():
        xn_ref[...] = xn_next[...]

    o_ref[...] = xbuf[...]

    @pl.when(has_next)
    def _():
        x_copy(i + 1).start()

    def tile_pair(jj, carry):
        for slot in (0, 1):
            j = 2 * jj + slot
            nxt = lax.rem(j + 1, n_tiles)

            @pl.when(jnp.logical_or(j + 1 < n_tiles, has_next))
            def _():
                for cp in tile_copies(nxt, 1 - slot):
                    cp.start()

            if slot == 1:
                @pl.when(jnp.logical_and(jj == 0, has_next))
                def _():
                    x_copy(i + 1).wait()

            for cp in tile_copies(j, slot):
                cp.wait()
            if slot == 1:
                rows = pl.ds(pl.multiple_of(jj * piece, piece), piece)
                for r0 in range(0, piece, 128):
                    r = pl.ds(rows.start + r0, 128)
                    xr = xbuf[r, :]
                    ms = jnp.mean(xr * xr, axis=-1, keepdims=True)
                    xn_next[r, :] = (xr * lax.rsqrt(ms + RMS_EPS) * g).astype(BF16)
            h = _dot(xn_ref[...], wu_buf[slot].astype(BF16))
            h = jnp.square(jnp.maximum(h, 0.0)).astype(BF16)
            o_ref[...] += _dot(h, wd_buf[slot].astype(BF16))
        return carry

    lax.fori_loop(0, n_tiles // 2, tile_pair, 0)

    if final_norm:
        _rmsnorm_rows(o_ref, gf_ref, o_ref)


def _mlp(x, g, w_up, w_down, layer, g_final, final_norm, tm, tf):
    t, d = x.shape
    f = w_up.shape[2]
    assert f % (2 * tf) == 0
    return pl.pallas_call(
        functools.partial(_mlp_kernel, layer=layer, final_norm=final_norm),
        out_shape=jax.ShapeDtypeStruct((t, d), F32),
        grid=(t // tm,),
        in_specs=[
            pl.BlockSpec(memory_space=pl.ANY),
            pl.BlockSpec((1, d), lambda i: (0, 0)),
            pl.BlockSpec(memory_space=pl.ANY),
            pl.BlockSpec(memory_space=pl.ANY),
            pl.BlockSpec((1, d), lambda i: (0, 0)),
        ],
        out_specs=pl.BlockSpec((tm, d), lambda i: (i, 0)),
        scratch_shapes=[
            pltpu.VMEM((tm, d), F32),
            pltpu.VMEM((tm, d), BF16),
            pltpu.VMEM((tm, d), BF16),
            pltpu.VMEM((2, d, tf), F32),
            pltpu.VMEM((2, tf, d), F32),
            pltpu.SemaphoreType.DMA((2, 2)),
            pltpu.SemaphoreType.DMA((1,)),
        ],
        compiler_params=_params(("arbitrary",), 58),
        name="mlp",
    )(x, g.reshape(1, d), w_up, w_down, g_final.reshape(1, d))


def kernel(x, mem, mem_norm_g, attn_norm_g, w_in_gla, w_gate_up, b_gate, gla_out_norm_g,
           w_in_swa, sinks, w_mem_kv, w_out, mlp_norm_g, w_up, w_down, final_norm_g):
    batch, seq, d = x.shape
    t = batch * seq
    xf = x.reshape(t, d)
    assert X_WIDTH == IN_TILE and SWA_KV_HEADS == 3

    mkv = _norm_matmul(mem.reshape(batch * MEM_LEN, d), mem_norm_g, w_mem_kv,
                       tm=batch * MEM_LEN)

    w_in_gla_t = jnp.swapaxes(w_in_gla, 1, 2)

    for i in range(DEPTH):
        j = i // 2
        if i % 2 == 0:
            proj = _in_proj(xf, attn_norm_g[i], w_in_gla_t, j, GLA_PIECES, GLA_CHUNKS,
                            transposed=True, tm=512)
            w_gate = jnp.pad(w_gate_up[j], ((0, GLA_RANK_PAD - GLA_GATE_RANK), (0, 0))).astype(BF16)
            o_mix = _gla_mixer(proj, w_gate, b_gate[j], gla_out_norm_g[j], batch, seq,
                               ts=1024, hpb=4)
            xq_block = (2 * GLA_V + 2 * GLA_K) // X_WIDTH
        else:
            proj = _in_proj(xf, attn_norm_g[i], w_in_swa, j, SWA_PIECES, SWA_CHUNKS,
                            transposed=False, tm=512)
            o_mix = _swa_mixer(proj, sinks[j], batch, seq, SWA_BLOCKS_PER_STEP)
            xq_block = SWA_Q // X_WIDTH
        o_mem = _mem_attention(proj, xq_block, mkv, i, batch, seq, tm=1024)
        xf = _out_proj(o_mix, o_mem, w_out, i, xf, tm=512)
        xf = _mlp(xf, mlp_norm_g[i], w_up, w_down, i, final_norm_g,
                  final_norm=(i == DEPTH - 1), tm=1024, tf=512)
    return xf.reshape(batch, seq, d)
```

```python
import functools

import jax
import jax.numpy as jnp
from jax import lax
from jax.experimental import pallas as pl
from jax.experimental.pallas import tpu as pltpu

F32 = jnp.float32
BF16 = jnp.bfloat16

D_MODEL = 2048
DEPTH = 4
MEM_LEN = 256
RMS_EPS = 1e-5
NEG_INF = -1e30

X_HEADS = 4
X_HEAD_DIM = 128
X_WIDTH = X_HEADS * X_HEAD_DIM

GLA_HEADS = 4
GLA_DK = 256
GLA_DV = 384
GLA_K = GLA_HEADS * GLA_DK
GLA_V = GLA_HEADS * GLA_DV
GLA_GATE_RANK = 16
GLA_TAU = 16.0
GLA_CHUNK = 64
GLA_SUB = 256
GLA_SUBS_PER_ITER = 1
GLA_RANK_PAD = 128
IN_TILE = 512
GLA_PROJ = 2 * GLA_V + 2 * GLA_K + X_WIDTH + GLA_RANK_PAD
GLA_PIECES = (
    tuple((2 * GLA_K + i * IN_TILE, i * IN_TILE, IN_TILE) for i in range(2 * GLA_V // IN_TILE))
    + tuple((i * IN_TILE, 2 * GLA_V + i * IN_TILE, IN_TILE) for i in range(2 * GLA_K // IN_TILE))
    + ((2 * GLA_K + 2 * GLA_V + GLA_GATE_RANK, 2 * GLA_K + 2 * GLA_V, X_WIDTH),
       (2 * GLA_K + 2 * GLA_V, 2 * GLA_K + 2 * GLA_V + X_WIDTH, GLA_GATE_RANK)))
GLA_CHUNKS = (0, 2048, 4096, GLA_PROJ)

SWA_HEAD_DIM = 64
SWA_Q_HEADS = 24
SWA_GROUP = 8
SWA_KV_HEADS = 3
SWA_WINDOW = 128
SWA_BLOCK = 128
SWA_Q = SWA_Q_HEADS * SWA_HEAD_DIM
SWA_KV = SWA_KV_HEADS * SWA_HEAD_DIM
SWA_PROJ = SWA_Q + X_WIDTH + IN_TILE
SWA_PIECES = (
    tuple((i * IN_TILE, i * IN_TILE, IN_TILE) for i in range(SWA_Q // IN_TILE))
    + ((SWA_Q + 2 * SWA_KV, SWA_Q, X_WIDTH), (SWA_Q, SWA_Q + X_WIDTH, IN_TILE)))
SWA_CHUNKS = (0, 1024, 2048, SWA_PROJ)
SWA_BLOCKS_PER_STEP = 2

D_FF = 4 * D_MODEL
MLP_ROW_PREFETCH_PAIR = 2

LANES = 128
BF16_SUBLANES = 16
MIB = 1024 * 1024


def _params(semantics, vmem_mib):
    return pltpu.CompilerParams(dimension_semantics=semantics,
                                vmem_limit_bytes=vmem_mib * MIB)


def _rmsnorm_rows(x_ref, g_ref, dst_ref, chunk=256):
    g = g_ref[...]

    def body(i, carry):
        r = pl.ds(pl.multiple_of(i * chunk, chunk), chunk)
        x = x_ref[r, :]
        ms = jnp.mean(x * x, axis=-1, keepdims=True)
        dst_ref[r, :] = (x * lax.rsqrt(ms + RMS_EPS) * g).astype(dst_ref.dtype)
        return carry

    lax.fori_loop(0, x_ref.shape[0] // chunk, body, 0)


def _dot(a, b):
    return jnp.dot(a, b, preferred_element_type=F32)


def _dot_t0(a, b):
    return lax.dot_general(a, b, (((0,), (0,)), ((), ())), preferred_element_type=F32)


def _dot_t1(a, b):
    return lax.dot_general(a, b, (((1,), (1,)), ((), ())), preferred_element_type=F32)


def _split_bf16(x):
    hi = x.astype(BF16)
    lo = (x - hi.astype(F32)).astype(BF16)
    return hi, lo


def _round_robin(generators):
    pending = list(generators)
    while pending:
        pending = [gen for gen in pending if next(gen, True) is None]


def _row_tiles(x_hbm, xbuf, sem):
    i = pl.program_id(0)
    tm = xbuf.shape[0]

    def copy(idx):
        rows = pl.ds(pl.multiple_of(idx * tm, tm), tm)
        return pltpu.make_async_copy(x_hbm.at[rows, :], xbuf, sem)

    def start_first():
        @pl.when(i == 0)
        def _():
            copy(0).start()

    def wait():
        copy(i).wait()

    def request_next():
        @pl.when(i + 1 < pl.num_programs(0))
        def _():
            copy(i + 1).start()

    return start_first, wait, request_next


def _norm_matmul_kernel(x_ref, g_ref, w_ref, o_ref, xn_ref):
    @pl.when(pl.program_id(1) == 0)
    def _():
        _rmsnorm_rows(x_ref, g_ref, xn_ref)

    o_ref[...] = _dot(xn_ref[...], w_ref[...].astype(BF16)).astype(o_ref.dtype)


def _norm_matmul(x, g, w, tm):
    t, d = x.shape
    layers, _, n1 = w.shape
    return pl.pallas_call(
        _norm_matmul_kernel,
        out_shape=jax.ShapeDtypeStruct((t, layers * n1), BF16),
        grid=(t // tm, layers),
        in_specs=[
            pl.BlockSpec((tm, d), lambda i, j: (i, 0)),
            pl.BlockSpec((1, d), lambda i, j: (0, 0)),
            pl.BlockSpec((None, d, n1), lambda i, j: (j, 0, 0)),
        ],
        out_specs=pl.BlockSpec((tm, n1), lambda i, j: (i, j)),
        scratch_shapes=[pltpu.VMEM((tm, d), BF16)],
        compiler_params=_params(("parallel", "arbitrary"), 52),
        name="norm_matmul",
    )(x, g.reshape(1, d), w)


def _in_proj_kernel(x_hbm, g_ref, w_hbm, o_ref, xbuf, xn_ref, wres, wbuf, sems,
                    *, layer, pieces, transposed, chunks):
    start_first_rows, wait_rows, request_next_rows = _row_tiles(x_hbm, xbuf, sems.at[0])
    start_first_rows()

    def piece_copy(p, slot):
        src, _, width = pieces[p]
        if transposed:
            return pltpu.make_async_copy(w_hbm.at[layer, pl.ds(src, width), :],
                                         wbuf.at[slot, pl.ds(0, width), :], sems.at[1 + slot])
        return pltpu.make_async_copy(w_hbm.at[layer, :, pl.ds(src, width)],
                                     wbuf.at[slot, :, pl.ds(0, width)], sems.at[1 + slot])

    @pl.when(pl.program_id(0) == 0)
    def _():
        piece_copy(0, 0).start()
        for p, (_, dst, width) in enumerate(pieces):
            slot = p % 2
            if p + 1 < len(pieces):
                piece_copy(p + 1, 1 - slot).start()
            piece_copy(p, slot).wait()
            if transposed:
                out_w = -(-width // LANES) * LANES
                tile = wbuf[slot, 0:out_w, :]
                if width < out_w:
                    r = lax.broadcasted_iota(jnp.int32, tile.shape, 0)
                    tile = jnp.where(r < width, tile, 0.0)
                wres[:, dst:dst + out_w] = tile.T.astype(BF16)
            else:
                wres[:, dst:dst + width] = wbuf[slot, :, 0:width].astype(BF16)

    wait_rows()
    _rmsnorm_rows(xbuf, g_ref, xn_ref)
    request_next_rows()
    for n0, n1 in zip(chunks[:-1], chunks[1:]):
        o_ref[:, n0:n1] = _dot(xn_ref[...], wres[:, n0:n1]).astype(o_ref.dtype)


def _in_proj(x, g, w, layer, pieces, chunks, transposed, tm):
    t, d = x.shape
    n_out = chunks[-1]
    stage = (2, IN_TILE, d) if transposed else (2, d, IN_TILE)
    return pl.pallas_call(
        functools.partial(_in_proj_kernel, layer=layer, pieces=pieces, transposed=transposed,
                          chunks=chunks),
        out_shape=jax.ShapeDtypeStruct((t, n_out), BF16),
        grid=(t // tm,),
        in_specs=[
            pl.BlockSpec(memory_space=pl.ANY),
            pl.BlockSpec((1, d), lambda i: (0, 0)),
            pl.BlockSpec(memory_space=pl.ANY),
        ],
        out_specs=pl.BlockSpec((tm, n_out), lambda i: (i, 0)),
        scratch_shapes=[
            pltpu.VMEM((tm, d), F32),
            pltpu.VMEM((tm, d), BF16),
            pltpu.VMEM((d, n_out), BF16),
            pltpu.VMEM(stage, F32),
            pltpu.SemaphoreType.DMA((3,)),
        ],
        compiler_params=_params(("arbitrary",), 60),
        name="in_proj",
    )(x, g.reshape(1, d), w)


def _gla_kernel(q_ref, k_ref, v_ref, go_ref, glr_ref, wg_ref, bg_ref, gn_ref,
                o_ref, state_ref):
    @pl.when(pl.program_id(2) == 0)
    def _():
        state_ref[...] = jnp.zeros_like(state_ref)

    c, sub = GLA_CHUNK, GLA_SUB
    nc = sub // c
    shift = c.bit_length() - 1
    row = lax.broadcasted_iota(jnp.int32, (sub, sub), 0)
    col = lax.broadcasted_iota(jnp.int32, (sub, sub), 1)
    causal = (jnp.right_shift(row, shift) == jnp.right_shift(col, shift)) & (row >= col)
    tri = jnp.where(causal, 1.0, 0.0).astype(BF16)
    prow = lax.broadcasted_iota(jnp.int32, (BF16_SUBLANES, nc * LANES), 0)
    pcol = lax.broadcasted_iota(jnp.int32, (BF16_SUBLANES, nc * LANES), 1)
    sel = jnp.where((prow < 2 * nc) & ((prow & (nc - 1)) == pcol // LANES), 1.0, 0.0).astype(BF16)
    gn = gn_ref[...]
    heads = q_ref.shape[1] // GLA_DK

    def head_sub_block(hh, r):
        ks = slice(hh * GLA_DK, (hh + 1) * GLA_DK)
        vs = slice(hh * GLA_DV, (hh + 1) * GLA_DV)
        pre = _dot(glr_ref[r, :], wg_ref[:, ks]) + bg_ref[:, ks]
        yield
        log_sig = jnp.minimum(pre, 0.0) - jnp.log(1.0 + jnp.exp(-jnp.abs(pre)))
        log_a = log_sig * (1.0 / GLA_TAU)
        la_hi, la_lo = _split_bf16(log_a)
        b = _dot(tri, la_hi) + _dot(tri, la_lo)
        yield
        b_last = [b[(i + 1) * c - 1:(i + 1) * c, :] for i in range(nc)]
        bl_hi, bl_lo = _split_bf16(jnp.concatenate(b_last, axis=0))
        pad = jnp.zeros((BF16_SUBLANES - 2 * nc, GLA_DK), BF16)
        b_last_cols = _dot_t0(jnp.concatenate([bl_hi, bl_lo, pad], axis=0), sel)
        b_last_rows = jnp.concatenate(
            [jnp.broadcast_to(x, (c, GLA_DK)) for x in b_last], axis=0)

        q = q_ref[r, ks].astype(F32) * (GLA_DK ** -0.5)
        k = k_ref[r, ks].astype(F32)
        v = v_ref[r, vs]
        q_in = (q * jnp.exp(b)).astype(BF16)
        k_in = (k * jnp.exp(-b)).astype(BF16)
        a_raw = _dot_t1(q_in, k_in)
        k_out = (k * jnp.exp(b_last_rows - b)).astype(BF16)
        kv = [_dot_t0(k_out[i * c:(i + 1) * c], v[i * c:(i + 1) * c]) for i in range(nc)]
        yield
        a = jnp.where(causal, a_raw, 0.0).astype(BF16)
        o_intra = _dot(a, v)
        decays = []
        for i in range(nc):
            decay = jnp.exp(b_last_cols[:, i * LANES:(i + 1) * LANES])
            decays.append(jnp.concatenate([decay] * (GLA_DV // LANES), axis=1))
        yield
        state = state_ref[hh]
        outs = []
        for i in range(nc):
            cs = slice(i * c, (i + 1) * c)
            outs.append(o_intra[cs] + _dot(q_in[cs], state.astype(BF16)))
            state = decays[i] * state + kv[i]
        state_ref[hh] = state
        yield
        o = jnp.concatenate(outs, axis=0)
        ms = jnp.mean(o * o, axis=-1, keepdims=True)
        y = o * lax.rsqrt(ms + RMS_EPS) * gn
        go = go_ref[r, vs].astype(F32)
        o_ref[r, vs] = (y * (go * jax.nn.sigmoid(go))).astype(o_ref.dtype)

    unroll = GLA_SUBS_PER_ITER

    def sub_blocks(si, carry):
        rows = [pl.ds(pl.multiple_of((si * unroll + u) * sub, sub), sub) for u in range(unroll)]
        _round_robin(head_sub_block(hh, r) for r in rows for hh in range(heads))
        return carry

    lax.fori_loop(0, q_ref.shape[0] // (sub * unroll), sub_blocks, 0)


def _gla_mixer(proj, w_gate, b_gate, out_g, batch, seq, ts, hpb):
    t = proj.shape[0]
    ns = seq // ts
    kw, vw = hpb * GLA_DK, hpb * GLA_DV
    vb, gb = 0, GLA_V // vw
    qb, kb = 2 * GLA_V // kw, (2 * GLA_V + GLA_K) // kw
    rb = (2 * GLA_V + 2 * GLA_K + X_WIDTH) // GLA_RANK_PAD

    def rows(b, h, c):
        return b * ns + c

    return pl.pallas_call(
        _gla_kernel,
        out_shape=jax.ShapeDtypeStruct((t, GLA_V), BF16),
        grid=(batch, GLA_HEADS // hpb, ns),
        in_specs=[
            pl.BlockSpec((ts, kw), lambda b, h, c: (rows(b, h, c), qb + h)),
            pl.BlockSpec((ts, kw), lambda b, h, c: (rows(b, h, c), kb + h)),
            pl.BlockSpec((ts, vw), lambda b, h, c: (rows(b, h, c), vb + h)),
            pl.BlockSpec((ts, vw), lambda b, h, c: (rows(b, h, c), gb + h)),
            pl.BlockSpec((ts, GLA_RANK_PAD), lambda b, h, c: (rows(b, h, c), rb)),
            pl.BlockSpec((GLA_RANK_PAD, kw), lambda b, h, c: (0, h)),
            pl.BlockSpec((1, kw), lambda b, h, c: (0, h)),
            pl.BlockSpec((1, GLA_DV), lambda b, h, c: (0, 0)),
        ],
        out_specs=pl.BlockSpec((ts, vw), lambda b, h, c: (rows(b, h, c), h)),
        scratch_shapes=[pltpu.VMEM((hpb, GLA_DK, GLA_DV), F32)],
        compiler_params=_params(("parallel", "parallel", "arbitrary"), 40),
        name="gla_mixer",
    )(proj, proj, proj, proj, proj, w_gate, b_gate.reshape(1, GLA_K),
      out_g.reshape(1, GLA_DV))


def _swa_kernel(sinks_ref, q_ref, kvp_ref, kvc_ref, o_ref, bias_ref):
    n = pl.program_id(1)
    blk = SWA_BLOCK
    hd = SWA_HEAD_DIM
    n_sub = q_ref.shape[0] // blk

    @pl.when(n == 0)
    def _():
        qpos = lax.broadcasted_iota(jnp.int32, (blk, 2 * blk), 0) + blk
        kpos = lax.broadcasted_iota(jnp.int32, (blk, 2 * blk), 1)
        dist = qpos - kpos
        window = (dist >= 0) & (dist < SWA_WINDOW)
        distf = dist.astype(F32)
        for h in range(SWA_Q_HEADS):
            slope = 2.0 ** (-8.0 * (h + 1) / SWA_Q_HEADS)
            bias_ref[0, h] = jnp.where(window, -slope * distf, NEG_INF)
            bias_ref[1, h] = jnp.where(window & (kpos >= blk), -slope * distf, NEG_INF)

    ones = jnp.ones((2 * blk, hd), BF16)
    low_half = lax.broadcasted_iota(jnp.int32, (blk, 2 * hd), 1) < hd
    half = SWA_GROUP // 2
    first_table = jnp.where(n == 0, 1, 0)

    def kv_group(sb, kv):
        rows = slice(sb * blk, (sb + 1) * blk)
        prev = kvp_ref if sb == 0 else kvc_ref.at[(sb - 1) * blk:sb * blk, :]
        table = first_table if sb == 0 else 0
        ks = slice(kv * hd, (kv + 1) * hd)
        vs = slice(SWA_KV + kv * hd, SWA_KV + (kv + 1) * hd)
        kb = jnp.concatenate([prev[:, ks], kvc_ref[rows, ks]], axis=0)
        vb = jnp.concatenate([prev[:, vs], kvc_ref[rows, vs]], axis=0)
        v_even = jnp.concatenate([vb, ones], axis=1)
        v_odd = jnp.concatenate([ones, vb], axis=1)
        first = kv * SWA_GROUP
        heads = list(range(first, first + SWA_GROUP, 2)) + list(range(first + 1, first + SWA_GROUP, 2))
        qs = jnp.concatenate([q_ref[rows, h * hd:(h + 1) * hd] for h in heads], axis=0)
        s = _dot_t1(qs * (hd ** -0.5), kb)
        yield

        def softmax_half(lo):
            ps, es = [], []
            for g in range(lo, lo + half):
                h = heads[g]
                sg = s[g * blk:(g + 1) * blk] + bias_ref[table, h]
                m = jnp.maximum(jnp.max(sg, axis=-1, keepdims=True), sinks_ref[h])
                ps.append(jnp.exp(sg - m).astype(BF16))
                es.append(jnp.exp(sinks_ref[h] - m))
            return jnp.concatenate(ps, axis=0), es

        p_even, e_even = softmax_half(0)
        pv_even = _dot(p_even, v_even)
        yield
        p_odd, e_odd = softmax_half(half)
        pv_odd = _dot(p_odd, v_odd)
        yield
        for t in range(half):
            prows = slice(t * blk, (t + 1) * blk)
            pe, po = pv_even[prows], pv_odd[prows]
            num = jnp.where(low_half, pe, po)
            den = pltpu.roll(jnp.where(low_half, po, pe), hd, axis=1)
            den = den + jnp.where(low_half, e_even[t], e_odd[t])
            pair = slice((first + 2 * t) * hd, (first + 2 * t + 2) * hd)
            o_ref[rows, pair] = (num / den).astype(o_ref.dtype)

    for sb in range(n_sub):
        groups = [kv_group(sb, kv) for kv in range(SWA_KV_HEADS)]
        order = [0, 1, 0, 0]
        for g in range(1, len(groups)):
            order += ([g + 1] if g + 1 < len(groups) else []) + [g, g - 1, g]
        order.append(len(groups) - 1)
        for g in order:
            next(groups[g], None)


def _swa_mixer(proj, sinks, batch, seq, blocks_per_step):
    t = proj.shape[0]
    nb = seq // SWA_BLOCK
    bps = blocks_per_step
    ns = nb // bps
    kvb = (SWA_Q + X_WIDTH) // IN_TILE
    return pl.pallas_call(
        _swa_kernel,
        out_shape=jax.ShapeDtypeStruct((t, SWA_Q), BF16),
        grid=(batch, ns),
        in_specs=[
            pl.BlockSpec(memory_space=pltpu.SMEM),
            pl.BlockSpec((bps * SWA_BLOCK, SWA_Q), lambda b, n: (b * ns + n, 0)),
            pl.BlockSpec((SWA_BLOCK, IN_TILE),
                         lambda b, n: (b * nb + jnp.maximum(bps * n - 1, 0), kvb)),
            pl.BlockSpec((bps * SWA_BLOCK, IN_TILE), lambda b, n: (b * ns + n, kvb)),
        ],
        out_specs=pl.BlockSpec((bps * SWA_BLOCK, SWA_Q), lambda b, n: (b * ns + n, 0)),
        scratch_shapes=[pltpu.VMEM((2, SWA_Q_HEADS, SWA_BLOCK, 2 * SWA_BLOCK), F32)],
        compiler_params=_params(("parallel", "arbitrary"), 40),
        name="swa_mixer",
    )(sinks, proj, proj, proj)


def _memattn_kernel(xq_ref, mk_ref, mv_ref, o_ref):
    hd = X_HEAD_DIM
    ones = jnp.ones((MEM_LEN, hd), BF16)

    def head(h):
        hs = slice(h * hd, (h + 1) * hd)
        s = _dot_t1(xq_ref[:, hs], mk_ref[:, hs]) * (hd ** -0.5)
        yield
        m = jnp.max(s, axis=-1, keepdims=True)
        p = jnp.exp(s - m).astype(BF16)
        pv = _dot(p, jnp.concatenate([mv_ref[:, hs], ones], axis=1))
        yield
        o_ref[:, hs] = (pv[:, :hd] / pv[:, hd:]).astype(o_ref.dtype)

    _round_robin(head(h) for h in range(X_HEADS))


def _mem_attention(proj, xq_block, mkv, layer, batch, seq, tm):
    t = proj.shape[0]
    ns = seq // tm
    return pl.pallas_call(
        _memattn_kernel,
        out_shape=jax.ShapeDtypeStruct((t, X_WIDTH), BF16),
        grid=(batch, ns),
        in_specs=[
            pl.BlockSpec((tm, X_WIDTH), lambda b, s: (b * ns + s, xq_block)),
            pl.BlockSpec((MEM_LEN, X_WIDTH), lambda b, s: (b, 2 * layer)),
            pl.BlockSpec((MEM_LEN, X_WIDTH), lambda b, s: (b, 2 * layer + 1)),
        ],
        out_specs=pl.BlockSpec((tm, X_WIDTH), lambda b, s: (b * ns + s, 0)),
        compiler_params=_params(("parallel", "arbitrary"), 32),
        name="mem_attention",
    )(proj, mkv, mkv)


def _out_proj_kernel(a_ref, b_ref, wa_ref, wb_ref, x_ref, o_ref):
    acc = _dot(a_ref[...], wa_ref[...].astype(BF16)) + _dot(b_ref[...], wb_ref[...].astype(BF16))
    o_ref[...] = x_ref[...] + acc


def _out_proj(o_mix, o_mem, w_out, layer, x, tm):
    t, d = x.shape
    ka, kb = o_mix.shape[1], o_mem.shape[1]
    once = pl.Buffered(1)
    return pl.pallas_call(
        _out_proj_kernel,
        out_shape=jax.ShapeDtypeStruct((t, d), F32),
        grid=(t // tm,),
        in_specs=[
            pl.BlockSpec((tm, ka), lambda i: (i, 0)),
            pl.BlockSpec((tm, kb), lambda i: (i, 0)),
            pl.BlockSpec((None, ka, d), lambda i: (layer, 0, 0), pipeline_mode=once),
            pl.BlockSpec((None, kb, d), lambda i: (layer, ka // kb, 0), pipeline_mode=once),
            pl.BlockSpec((tm, d), lambda i: (i, 0)),
        ],
        out_specs=pl.BlockSpec((tm, d), lambda i: (i, 0)),
        compiler_params=_params(("parallel",), 56),
        name="out_proj",
    )(o_mix, o_mem, w_out, w_out, x)


def _mlp_kernel(x_hbm, g_ref, wu_hbm, wd_hbm, gf_ref, o_ref, xbuf, xn_ref, wu_buf, wd_buf,
                sems, xsem, *, layer, final_norm):
    tf = wu_buf.shape[2]
    n_tiles = wu_hbm.shape[2] // tf

    def tile_copies(j, slot):
        off = pl.multiple_of(j * tf, tf)
        return (
            pltpu.make_async_copy(wu_hbm.at[layer, :, pl.ds(off, tf)], wu_buf.at[slot],
                                  sems.at[0, slot]),
            pltpu.make_async_copy(wd_hbm.at[layer, pl.ds(off, tf), :], wd_buf.at[slot],
                                  sems.at[1, slot]),
        )

    i = pl.program_id(0)
    has_next = i + 1 < pl.num_programs(0)
    start_first_rows, wait_rows, request_next_rows = _row_tiles(x_hbm, xbuf, xsem.at[0])
    start_first_rows()

    @pl.when(i == 0)
    def _():
        for cp in tile_copies(0, 0):
            cp.start()

    wait_rows()
    _rmsnorm_rows(xbuf, g_ref, xn_ref)
    o_ref[...] = xbuf[...]

    def tile_pair(jj, carry):
        for slot in (0, 1):
            j = 2 * jj + slot
            nxt = lax.rem(j + 1, n_tiles)

            @pl.when(jnp.logical_or(j + 1 < n_tiles, has_next))
            def _():
                for cp in tile_copies(nxt, 1 - slot):
                    cp.start()

            if slot == 0:
                @pl.when(jj == MLP_ROW_PREFETCH_PAIR)
                def _():
                    request_next_rows()

            for cp in tile_copies(j, slot):
                cp.wait()
            h = _dot(xn_ref[...], wu_buf[slot].astype(BF16))
            h = jnp.square(jnp.maximum(h, 0.0)).astype(BF16)
            o_ref[...] += _dot(h, wd_buf[slot].astype(BF16))
        return carry

    lax.fori_loop(0, n_tiles // 2, tile_pair, 0)

    if final_norm:
        _rmsnorm_rows(o_ref, gf_ref, o_ref)


def _mlp(x, g, w_up, w_down, layer, g_final, final_norm, tm, tf):
    t, d = x.shape
    f = w_up.shape[2]
    assert f % (2 * tf) == 0
    return pl.pallas_call(
        functools.partial(_mlp_kernel, layer=layer, final_norm=final_norm),
        out_shape=jax.ShapeDtypeStruct((t, d), F32),
        grid=(t // tm,),
        in_specs=[
            pl.BlockSpec(memory_space=pl.ANY),
            pl.BlockSpec((1, d), lambda i: (0, 0)),
            pl.BlockSpec(memory_space=pl.ANY),
            pl.BlockSpec(memory_space=pl.ANY),
            pl.BlockSpec((1, d), lambda i: (0, 0)),
        ],
        out_specs=pl.BlockSpec((tm, d), lambda i: (i, 0)),
        scratch_shapes=[
            pltpu.VMEM((tm, d), F32),
            pltpu.VMEM((tm, d), BF16),
            pltpu.VMEM((2, d, tf), F32),
            pltpu.VMEM((2, tf, d), F32),
            pltpu.SemaphoreType.DMA((2, 2)),
            pltpu.SemaphoreType.DMA((1,)),
        ],
        compiler_params=_params(("arbitrary",), 58),
        name="mlp",
    )(x, g.reshape(1, d), w_up, w_down, g_final.reshape(1, d))


def kernel(x, mem, mem_norm_g, attn_norm_g, w_in_gla, w_gate_up, b_gate, gla_out_norm_g,
           w_in_swa, sinks, w_mem_kv, w_out, mlp_norm_g, w_up, w_down, final_norm_g):
    batch, seq, d = x.shape
    t = batch * seq
    xf = x.reshape(t, d)
    assert X_WIDTH == IN_TILE and SWA_KV_HEADS == 3

    mkv = _norm_matmul(mem.reshape(batch * MEM_LEN, d), mem_norm_g, w_mem_kv,
                       tm=batch * MEM_LEN)

    w_in_gla_t = jnp.swapaxes(w_in_gla, 1, 2)

    for i in range(DEPTH):
        j = i // 2
        if i % 2 == 0:
            proj = _in_proj(xf, attn_norm_g[i], w_in_gla_t, j, GLA_PIECES, GLA_CHUNKS,
                            transposed=True, tm=512)
            w_gate = jnp.pad(w_gate_up[j], ((0, GLA_RANK_PAD - GLA_GATE_RANK), (0, 0))).astype(BF16)
            o_mix = _gla_mixer(proj, w_gate, b_gate[j], gla_out_norm_g[j], batch, seq,
                               ts=1024, hpb=4)
            xq_block = (2 * GLA_V + 2 * GLA_K) // X_WIDTH
        else:
            proj = _in_proj(xf, attn_norm_g[i], w_in_swa, j, SWA_PIECES, SWA_CHUNKS,
                            transposed=False, tm=512)
            o_mix = _swa_mixer(proj, sinks[j], batch, seq, SWA_BLOCKS_PER_STEP)
            xq_block = SWA_Q // X_WIDTH
        o_mem = _mem_attention(proj, xq_block, mkv, i, batch, seq, tm=1024)
        xf = _out_proj(o_mix, o_mem, w_out, i, xf, tm=512)
        xf = _mlp(xf, mlp_norm_g[i], w_up, w_down, i, final_norm_g,
                  final_norm=(i == DEPTH - 1), tm=1024, tf=512)
    return xf.reshape(batch, seq, d)
```

```python
import functools

import jax
import jax.numpy as jnp
from jax import lax
from jax.experimental import pallas as pl
from jax.experimental.pallas import tpu as pltpu

F32 = jnp.float32
BF16 = jnp.bfloat16

D_MODEL = 2048
DEPTH = 4
MEM_LEN = 256
RMS_EPS = 1e-5
NEG_INF = -1e30

X_HEADS = 4
X_HEAD_DIM = 128
X_WIDTH = X_HEADS * X_HEAD_DIM

GLA_HEADS = 4
GLA_DK = 256
GLA_DV = 384
GLA_K = GLA_HEADS * GLA_DK
GLA_V = GLA_HEADS * GLA_DV
GLA_GATE_RANK = 16
GLA_TAU = 16.0
GLA_CHUNK = 64
GLA_SUB = 256
GLA_SUBS_PER_ITER = 1
GLA_RANK_PAD = 128
IN_TILE = 512
GLA_PROJ = 2 * GLA_V + 2 * GLA_K + X_WIDTH + GLA_RANK_PAD
GLA_PIECES = (
    tuple((2 * GLA_K + i * IN_TILE, i * IN_TILE, IN_TILE) for i in range(2 * GLA_V // IN_TILE))
    + tuple((i * IN_TILE, 2 * GLA_V + i * IN_TILE, IN_TILE) for i in range(2 * GLA_K // IN_TILE))
    + ((2 * GLA_K + 2 * GLA_V + GLA_GATE_RANK, 2 * GLA_K + 2 * GLA_V, X_WIDTH),
       (2 * GLA_K + 2 * GLA_V, 2 * GLA_K + 2 * GLA_V + X_WIDTH, GLA_GATE_RANK)))
GLA_CHUNKS = (0, 2048, 4096, GLA_PROJ)

SWA_HEAD_DIM = 64
SWA_Q_HEADS = 24
SWA_GROUP = 8
SWA_KV_HEADS = 3
SWA_WINDOW = 128
SWA_BLOCK = 128
SWA_Q = SWA_Q_HEADS * SWA_HEAD_DIM
SWA_KV = SWA_KV_HEADS * SWA_HEAD_DIM
SWA_PROJ = SWA_Q + X_WIDTH + IN_TILE
SWA_PIECES = (
    tuple((i * IN_TILE, i * IN_TILE, IN_TILE) for i in range(SWA_Q // IN_TILE))
    + ((SWA_Q + 2 * SWA_KV, SWA_Q, X_WIDTH), (SWA_Q, SWA_Q + X_WIDTH, IN_TILE)))
SWA_CHUNKS = (0, 1024, 2048, SWA_PROJ)
SWA_BLOCKS_PER_STEP = 4

D_FF = 4 * D_MODEL
MLP_ROW_PREFETCH_PAIR = 2

LANES = 128
BF16_SUBLANES = 16
MIB = 1024 * 1024


def _params(semantics, vmem_mib):
    return pltpu.CompilerParams(dimension_semantics=semantics,
                                vmem_limit_bytes=vmem_mib * MIB)


def _rmsnorm_rows(x_ref, g_ref, dst_ref, chunk=256):
    g = g_ref[...]

    def body(i, carry):
        r = pl.ds(pl.multiple_of(i * chunk, chunk), chunk)
        x = x_ref[r, :]
        ms = jnp.mean(x * x, axis=-1, keepdims=True)
        dst_ref[r, :] = (x * lax.rsqrt(ms + RMS_EPS) * g).astype(dst_ref.dtype)
        return carry

    lax.fori_loop(0, x_ref.shape[0] // chunk, body, 0)


def _dot(a, b):
    return jnp.dot(a, b, preferred_element_type=F32)


def _dot_t0(a, b):
    return lax.dot_general(a, b, (((0,), (0,)), ((), ())), preferred_element_type=F32)


def _dot_t1(a, b):
    return lax.dot_general(a, b, (((1,), (1,)), ((), ())), preferred_element_type=F32)


def _split_bf16(x):
    hi = x.astype(BF16)
    lo = (x - hi.astype(F32)).astype(BF16)
    return hi, lo


def _round_robin(generators):
    pending = list(generators)
    while pending:
        pending = [gen for gen in pending if next(gen, True) is None]


def _row_tiles(x_hbm, xbuf, sem):
    i = pl.program_id(0)
    tm = xbuf.shape[0]

    def copy(idx):
        rows = pl.ds(pl.multiple_of(idx * tm, tm), tm)
        return pltpu.make_async_copy(x_hbm.at[rows, :], xbuf, sem)

    def start_first():
        @pl.when(i == 0)
        def _():
            copy(0).start()

    def wait():
        copy(i).wait()

    def request_next():
        @pl.when(i + 1 < pl.num_programs(0))
        def _():
            copy(i + 1).start()

    return start_first, wait, request_next


def _norm_matmul_kernel(x_ref, g_ref, w_ref, o_ref, xn_ref):
    @pl.when(pl.program_id(1) == 0)
    def _():
        _rmsnorm_rows(x_ref, g_ref, xn_ref)

    o_ref[...] = _dot(xn_ref[...], w_ref[...].astype(BF16)).astype(o_ref.dtype)


def _norm_matmul(x, g, w, tm):
    t, d = x.shape
    layers, _, n1 = w.shape
    return pl.pallas_call(
        _norm_matmul_kernel,
        out_shape=jax.ShapeDtypeStruct((t, layers * n1), BF16),
        grid=(t // tm, layers),
        in_specs=[
            pl.BlockSpec((tm, d), lambda i, j: (i, 0)),
            pl.BlockSpec((1, d), lambda i, j: (0, 0)),
            pl.BlockSpec((None, d, n1), lambda i, j: (j, 0, 0)),
        ],
        out_specs=pl.BlockSpec((tm, n1), lambda i, j: (i, j)),
        scratch_shapes=[pltpu.VMEM((tm, d), BF16)],
        compiler_params=_params(("parallel", "arbitrary"), 52),
        name="norm_matmul",
    )(x, g.reshape(1, d), w)


def _in_proj_kernel(x_hbm, g_ref, w_hbm, o_ref, xbuf, xn_ref, wres, wbuf, sems,
                    *, layer, pieces, transposed, chunks):
    start_first_rows, wait_rows, request_next_rows = _row_tiles(x_hbm, xbuf, sems.at[0])
    start_first_rows()

    def piece_copy(p, slot):
        src, _, width = pieces[p]
        if transposed:
            return pltpu.make_async_copy(w_hbm.at[layer, pl.ds(src, width), :],
                                         wbuf.at[slot, pl.ds(0, width), :], sems.at[1 + slot])
        return pltpu.make_async_copy(w_hbm.at[layer, :, pl.ds(src, width)],
                                     wbuf.at[slot, :, pl.ds(0, width)], sems.at[1 + slot])

    @pl.when(pl.program_id(0) == 0)
    def _():
        piece_copy(0, 0).start()
        for p, (_, dst, width) in enumerate(pieces):
            slot = p % 2
            if p + 1 < len(pieces):
                piece_copy(p + 1, 1 - slot).start()
            piece_copy(p, slot).wait()
            if transposed:
                out_w = -(-width // LANES) * LANES
                tile = wbuf[slot, 0:out_w, :]
                if width < out_w:
                    r = lax.broadcasted_iota(jnp.int32, tile.shape, 0)
                    tile = jnp.where(r < width, tile, 0.0)
                wres[:, dst:dst + out_w] = tile.T.astype(BF16)
            else:
                wres[:, dst:dst + width] = wbuf[slot, :, 0:width].astype(BF16)

    wait_rows()
    _rmsnorm_rows(xbuf, g_ref, xn_ref)
    request_next_rows()
    for n0, n1 in zip(chunks[:-1], chunks[1:]):
        o_ref[:, n0:n1] = _dot(xn_ref[...], wres[:, n0:n1]).astype(o_ref.dtype)


def _in_proj(x, g, w, layer, pieces, chunks, transposed, tm):
    t, d = x.shape
    n_out = chunks[-1]
    stage = (2, IN_TILE, d) if transposed else (2, d, IN_TILE)
    return pl.pallas_call(
        functools.partial(_in_proj_kernel, layer=layer, pieces=pieces, transposed=transposed,
                          chunks=chunks),
        out_shape=jax.ShapeDtypeStruct((t, n_out), BF16),
        grid=(t // tm,),
        in_specs=[
            pl.BlockSpec(memory_space=pl.ANY),
            pl.BlockSpec((1, d), lambda i: (0, 0)),
            pl.BlockSpec(memory_space=pl.ANY),
        ],
        out_specs=pl.BlockSpec((tm, n_out), lambda i: (i, 0)),
        scratch_shapes=[
            pltpu.VMEM((tm, d), F32),
            pltpu.VMEM((tm, d), BF16),
            pltpu.VMEM((d, n_out), BF16),
            pltpu.VMEM(stage, F32),
            pltpu.SemaphoreType.DMA((3,)),
        ],
        compiler_params=_params(("arbitrary",), 60),
        name="in_proj",
    )(x, g.reshape(1, d), w)


def _gla_kernel(q_ref, k_ref, v_ref, go_ref, glr_ref, wg_ref, bg_ref, gn_ref,
                o_ref, state_ref):
    @pl.when(pl.program_id(2) == 0)
    def _():
        state_ref[...] = jnp.zeros_like(state_ref)

    c, sub = GLA_CHUNK, GLA_SUB
    nc = sub // c
    shift = c.bit_length() - 1
    row = lax.broadcasted_iota(jnp.int32, (sub, sub), 0)
    col = lax.broadcasted_iota(jnp.int32, (sub, sub), 1)
    causal = (jnp.right_shift(row, shift) == jnp.right_shift(col, shift)) & (row >= col)
    tri = jnp.where(causal, 1.0, 0.0).astype(BF16)
    prow = lax.broadcasted_iota(jnp.int32, (BF16_SUBLANES, nc * LANES), 0)
    pcol = lax.broadcasted_iota(jnp.int32, (BF16_SUBLANES, nc * LANES), 1)
    sel = jnp.where((prow < 2 * nc) & ((prow & (nc - 1)) == pcol // LANES), 1.0, 0.0).astype(BF16)
    gn = gn_ref[...]
    heads = q_ref.shape[1] // GLA_DK

    def head_sub_block(hh, r):
        ks = slice(hh * GLA_DK, (hh + 1) * GLA_DK)
        vs = slice(hh * GLA_DV, (hh + 1) * GLA_DV)
        pre = _dot(glr_ref[r, :], wg_ref[:, ks]) + bg_ref[:, ks]
        yield
        log_sig = jnp.minimum(pre, 0.0) - jnp.log(1.0 + jnp.exp(-jnp.abs(pre)))
        log_a = log_sig * (1.0 / GLA_TAU)
        la_hi, la_lo = _split_bf16(log_a)
        b = _dot(tri, la_hi) + _dot(tri, la_lo)
        yield
        b_last = [b[(i + 1) * c - 1:(i + 1) * c, :] for i in range(nc)]
        bl_hi, bl_lo = _split_bf16(jnp.concatenate(b_last, axis=0))
        pad = jnp.zeros((BF16_SUBLANES - 2 * nc, GLA_DK), BF16)
        b_last_cols = _dot_t0(jnp.concatenate([bl_hi, bl_lo, pad], axis=0), sel)
        b_last_rows = jnp.concatenate(
            [jnp.broadcast_to(x, (c, GLA_DK)) for x in b_last], axis=0)

        q = q_ref[r, ks].astype(F32) * (GLA_DK ** -0.5)
        k = k_ref[r, ks].astype(F32)
        v = v_ref[r, vs]
        q_in = (q * jnp.exp(b)).astype(BF16)
        k_in = (k * jnp.exp(-b)).astype(BF16)
        a_raw = _dot_t1(q_in, k_in)
        k_out = (k * jnp.exp(b_last_rows - b)).astype(BF16)
        kv = [_dot_t0(k_out[i * c:(i + 1) * c], v[i * c:(i + 1) * c]) for i in range(nc)]
        yield
        a = jnp.where(causal, a_raw, 0.0).astype(BF16)
        o_intra = _dot(a, v)
        decays = []
        for i in range(nc):
            decay = jnp.exp(b_last_cols[:, i * LANES:(i + 1) * LANES])
            decays.append(jnp.concatenate([decay] * (GLA_DV // LANES), axis=1))
        yield
        state = state_ref[hh]
        outs = []
        for i in range(nc):
            cs = slice(i * c, (i + 1) * c)
            outs.append(o_intra[cs] + _dot(q_in[cs], state.astype(BF16)))
            state = decays[i] * state + kv[i]
        state_ref[hh] = state
        yield
        o = jnp.concatenate(outs, axis=0)
        ms = jnp.mean(o * o, axis=-1, keepdims=True)
        y = o * lax.rsqrt(ms + RMS_EPS) * gn
        go = go_ref[r, vs].astype(F32)
        o_ref[r, vs] = (y * (go * jax.nn.sigmoid(go))).astype(o_ref.dtype)

    unroll = GLA_SUBS_PER_ITER

    def sub_blocks(si, carry):
        rows = [pl.ds(pl.multiple_of((si * unroll + u) * sub, sub), sub) for u in range(unroll)]
        _round_robin(head_sub_block(hh, r) for r in rows for hh in range(heads))
        return carry

    lax.fori_loop(0, q_ref.shape[0] // (sub * unroll), sub_blocks, 0)


def _gla_mixer(proj, w_gate, b_gate, out_g, batch, seq, ts, hpb):
    t = proj.shape[0]
    ns = seq // ts
    kw, vw = hpb * GLA_DK, hpb * GLA_DV
    vb, gb = 0, GLA_V // vw
    qb, kb = 2 * GLA_V // kw, (2 * GLA_V + GLA_K) // kw
    rb = (2 * GLA_V + 2 * GLA_K + X_WIDTH) // GLA_RANK_PAD

    def rows(b, h, c):
        return b * ns + c

    return pl.pallas_call(
        _gla_kernel,
        out_shape=jax.ShapeDtypeStruct((t, GLA_V), BF16),
        grid=(batch, GLA_HEADS // hpb, ns),
        in_specs=[
            pl.BlockSpec((ts, kw), lambda b, h, c: (rows(b, h, c), qb + h)),
            pl.BlockSpec((ts, kw), lambda b, h, c: (rows(b, h, c), kb + h)),
            pl.BlockSpec((ts, vw), lambda b, h, c: (rows(b, h, c), vb + h)),
            pl.BlockSpec((ts, vw), lambda b, h, c: (rows(b, h, c), gb + h)),
            pl.BlockSpec((ts, GLA_RANK_PAD), lambda b, h, c: (rows(b, h, c), rb)),
            pl.BlockSpec((GLA_RANK_PAD, kw), lambda b, h, c: (0, h)),
            pl.BlockSpec((1, kw), lambda b, h, c: (0, h)),
            pl.BlockSpec((1, GLA_DV), lambda b, h, c: (0, 0)),
        ],
        out_specs=pl.BlockSpec((ts, vw), lambda b, h, c: (rows(b, h, c), h)),
        scratch_shapes=[pltpu.VMEM((hpb, GLA_DK, GLA_DV), F32)],
        compiler_params=_params(("parallel", "parallel", "arbitrary"), 40),
        name="gla_mixer",
    )(proj, proj, proj, proj, proj, w_gate, b_gate.reshape(1, GLA_K),
      out_g.reshape(1, GLA_DV))


def _swa_kernel(sinks_ref, q_ref, kvp_ref, kvc_ref, o_ref, bias_ref):
    n = pl.program_id(1)
    blk = SWA_BLOCK
    hd = SWA_HEAD_DIM
    n_sub = q_ref.shape[0] // blk

    @pl.when(n == 0)
    def _():
        qpos = lax.broadcasted_iota(jnp.int32, (blk, 2 * blk), 0) + blk
        kpos = lax.broadcasted_iota(jnp.int32, (blk, 2 * blk), 1)
        dist = qpos - kpos
        window = (dist >= 0) & (dist < SWA_WINDOW)
        distf = dist.astype(F32)
        for h in range(SWA_Q_HEADS):
            slope = 2.0 ** (-8.0 * (h + 1) / SWA_Q_HEADS)
            bias_ref[0, h] = jnp.where(window, -slope * distf, NEG_INF)
            bias_ref[1, h] = jnp.where(window & (kpos >= blk), -slope * distf, NEG_INF)

    ones = jnp.ones((2 * blk, hd), BF16)
    low_half = lax.broadcasted_iota(jnp.int32, (blk, 2 * hd), 1) < hd
    half = SWA_GROUP // 2
    first_table = jnp.where(n == 0, 1, 0)

    def kv_group(sb, kv):
        rows = slice(sb * blk, (sb + 1) * blk)
        prev = kvp_ref if sb == 0 else kvc_ref.at[(sb - 1) * blk:sb * blk, :]
        table = first_table if sb == 0 else 0
        ks = slice(kv * hd, (kv + 1) * hd)
        vs = slice(SWA_KV + kv * hd, SWA_KV + (kv + 1) * hd)
        kb = jnp.concatenate([prev[:, ks], kvc_ref[rows, ks]], axis=0)
        vb = jnp.concatenate([prev[:, vs], kvc_ref[rows, vs]], axis=0)
        v_even = jnp.concatenate([vb, ones], axis=1)
        v_odd = jnp.concatenate([ones, vb], axis=1)
        first = kv * SWA_GROUP
        heads = list(range(first, first + SWA_GROUP, 2)) + list(range(first + 1, first + SWA_GROUP, 2))
        qs = jnp.concatenate([q_ref[rows, h * hd:(h + 1) * hd] for h in heads], axis=0)
        s = _dot_t1(qs * (hd ** -0.5), kb)
        yield

        def softmax_half(lo):
            ps, es = [], []
            for g in range(lo, lo + half):
                h = heads[g]
                sg = s[g * blk:(g + 1) * blk] + bias_ref[table, h]
                m = jnp.maximum(jnp.max(sg, axis=-1, keepdims=True), sinks_ref[h])
                ps.append(jnp.exp(sg - m).astype(BF16))
                es.append(jnp.exp(sinks_ref[h] - m))
            return jnp.concatenate(ps, axis=0), es

        p_even, e_even = softmax_half(0)
        pv_even = _dot(p_even, v_even)
        yield
        p_odd, e_odd = softmax_half(half)
        pv_odd = _dot(p_odd, v_odd)
        yield
        for t in range(half):
            prows = slice(t * blk, (t + 1) * blk)
            pe, po = pv_even[prows], pv_odd[prows]
            num = jnp.where(low_half, pe, po)
            den = pltpu.roll(jnp.where(low_half, po, pe), hd, axis=1)
            den = den + jnp.where(low_half, e_even[t], e_odd[t])
            pair = slice((first + 2 * t) * hd, (first + 2 * t + 2) * hd)
            o_ref[rows, pair] = (num / den).astype(o_ref.dtype)

    for sb in range(n_sub):
        groups = [kv_group(sb, kv) for kv in range(SWA_KV_HEADS)]
        order = [0, 1, 0, 0]
        for g in range(1, len(groups)):
            order += ([g + 1] if g + 1 < len(groups) else []) + [g, g - 1, g]
        order.append(len(groups) - 1)
        for g in order:
            next(groups[g], None)


def _swa_mixer(proj, sinks, batch, seq, blocks_per_step):
    t = proj.shape[0]
    nb = seq // SWA_BLOCK
    bps = blocks_per_step
    ns = nb // bps
    kvb = (SWA_Q + X_WIDTH) // IN_TILE
    return pl.pallas_call(
        _swa_kernel,
        out_shape=jax.ShapeDtypeStruct((t, SWA_Q), BF16),
        grid=(batch, ns),
        in_specs=[
            pl.BlockSpec(memory_space=pltpu.SMEM),
            pl.BlockSpec((bps * SWA_BLOCK, SWA_Q), lambda b, n: (b * ns + n, 0)),
            pl.BlockSpec((SWA_BLOCK, IN_TILE),
                         lambda b, n: (b * nb + jnp.maximum(bps * n - 1, 0), kvb)),
            pl.BlockSpec((bps * SWA_BLOCK, IN_TILE), lambda b, n: (b * ns + n, kvb)),
        ],
        out_specs=pl.BlockSpec((bps * SWA_BLOCK, SWA_Q), lambda b, n: (b * ns + n, 0)),
        scratch_shapes=[pltpu.VMEM((2, SWA_Q_HEADS, SWA_BLOCK, 2 * SWA_BLOCK), F32)],
        compiler_params=_params(("parallel", "arbitrary"), 40),
        name="swa_mixer",
    )(sinks, proj, proj, proj)


def _memattn_kernel(xq_ref, mk_ref, mv_ref, o_ref):
    hd = X_HEAD_DIM
    ones = jnp.ones((MEM_LEN, hd), BF16)

    def head(h):
        hs = slice(h * hd, (h + 1) * hd)
        s = _dot_t1(xq_ref[:, hs], mk_ref[:, hs]) * (hd ** -0.5)
        yield
        m = jnp.max(s, axis=-1, keepdims=True)
        p = jnp.exp(s - m).astype(BF16)
        pv = _dot(p, jnp.concatenate([mv_ref[:, hs], ones], axis=1))
        yield
        o_ref[:, hs] = (pv[:, :hd] / pv[:, hd:]).astype(o_ref.dtype)

    _round_robin(head(h) for h in range(X_HEADS))


def _mem_attention(proj, xq_block, mkv, layer, batch, seq, tm):
    t = proj.shape[0]
    ns = seq // tm
    return pl.pallas_call(
        _memattn_kernel,
        out_shape=jax.ShapeDtypeStruct((t, X_WIDTH), BF16),
        grid=(batch, ns),
        in_specs=[
            pl.BlockSpec((tm, X_WIDTH), lambda b, s: (b * ns + s, xq_block)),
            pl.BlockSpec((MEM_LEN, X_WIDTH), lambda b, s: (b, 2 * layer)),
            pl.BlockSpec((MEM_LEN, X_WIDTH), lambda b, s: (b, 2 * layer + 1)),
        ],
        out_specs=pl.BlockSpec((tm, X_WIDTH), lambda b, s: (b * ns + s, 0)),
        compiler_params=_params(("parallel", "arbitrary"), 32),
        name="mem_attention",
    )(proj, mkv, mkv)


def _out_proj_kernel(a_ref, b_ref, wa_ref, wb_ref, x_ref, o_ref):
    acc = _dot(a_ref[...], wa_ref[...].astype(BF16)) + _dot(b_ref[...], wb_ref[...].astype(BF16))
    o_ref[...] = x_ref[...] + acc


def _out_proj(o_mix, o_mem, w_out, layer, x, tm):
    t, d = x.shape
    ka, kb = o_mix.shape[1], o_mem.shape[1]
    once = pl.Buffered(1)
    return pl.pallas_call(
        _out_proj_kernel,
        out_shape=jax.ShapeDtypeStruct((t, d), F32),
        grid=(t // tm,),
        in_specs=[
            pl.BlockSpec((tm, ka), lambda i: (i, 0)),
            pl.BlockSpec((tm, kb), lambda i: (i, 0)),
            pl.BlockSpec((None, ka, d), lambda i: (layer, 0, 0), pipeline_mode=once),
            pl.BlockSpec((None, kb, d), lambda i: (layer, ka // kb, 0), pipeline_mode=once),
            pl.BlockSpec((tm, d), lambda i: (i, 0)),
        ],
        out_specs=pl.BlockSpec((tm, d), lambda i: (i, 0)),
        compiler_params=_params(("parallel",), 56),
        name="out_proj",
    )(o_mix, o_mem, w_out, w_out, x)


def _mlp_kernel(x_hbm, g_ref, wu_hbm, wd_hbm, gf_ref, o_ref, xbuf, xn_ref, wu_buf, wd_buf,
                sems, xsem, *, layer, final_norm):
    tf = wu_buf.shape[2]
    n_tiles = wu_hbm.shape[2] // tf

    def tile_copies(j, slot):
        off = pl.multiple_of(j * tf, tf)
        return (
            pltpu.make_async_copy(wu_hbm.at[layer, :, pl.ds(off, tf)], wu_buf.at[slot],
                                  sems.at[0, slot]),
            pltpu.make_async_copy(wd_hbm.at[layer, pl.ds(off, tf), :], wd_buf.at[slot],
                                  sems.at[1, slot]),
        )

    i = pl.program_id(0)
    has_next = i + 1 < pl.num_programs(0)
    start_first_rows, wait_rows, request_next_rows = _row_tiles(x_hbm, xbuf, xsem.at[0])
    start_first_rows()

    @pl.when(i == 0)
    def _():
        for cp in tile_copies(0, 0):
            cp.start()

    wait_rows()
    _rmsnorm_rows(xbuf, g_ref, xn_ref)
    o_ref[...] = xbuf[...]

    def tile_pair(jj, carry):
        for slot in (0, 1):
            j = 2 * jj + slot
            nxt = lax.rem(j + 1, n_tiles)

            @pl.when(jnp.logical_or(j + 1 < n_tiles, has_next))
            def _():
                for cp in tile_copies(nxt, 1 - slot):
                    cp.start()

            if slot == 0:
                @pl.when(jj == MLP_ROW_PREFETCH_PAIR)
                def _():
                    request_next_rows()

            for cp in tile_copies(j, slot):
                cp.wait()
            h = _dot(xn_ref[...], wu_buf[slot].astype(BF16))
            h = jnp.square(jnp.maximum(h, 0.0)).astype(BF16)
            o_ref[...] += _dot(h, wd_buf[slot].astype(BF16))
        return carry

    lax.fori_loop(0, n_tiles // 2, tile_pair, 0)

    if final_norm:
        _rmsnorm_rows(o_ref, gf_ref, o_ref)


def _mlp(x, g, w_up, w_down, layer, g_final, final_norm, tm, tf):
    t, d = x.shape
    f = w_up.shape[2]
    assert f % (2 * tf) == 0
    return pl.pallas_call(
        functools.partial(_mlp_kernel, layer=layer, final_norm=final_norm),
        out_shape=jax.ShapeDtypeStruct((t, d), F32),
        grid=(t // tm,),
        in_specs=[
            pl.BlockSpec(memory_space=pl.ANY),
            pl.BlockSpec((1, d), lambda i: (0, 0)),
            pl.BlockSpec(memory_space=pl.ANY),
            pl.BlockSpec(memory_space=pl.ANY),
            pl.BlockSpec((1, d), lambda i: (0, 0)),
        ],
        out_specs=pl.BlockSpec((tm, d), lambda i: (i, 0)),
        scratch_shapes=[
            pltpu.VMEM((tm, d), F32),
            pltpu.VMEM((tm, d), BF16),
            pltpu.VMEM((2, d, tf), F32),
            pltpu.VMEM((2, tf, d), F32),
            pltpu.SemaphoreType.DMA((2, 2)),
            pltpu.SemaphoreType.DMA((1,)),
        ],
        compiler_params=_params(("arbitrary",), 58),
        name="mlp",
    )(x, g.reshape(1, d), w_up, w_down, g_final.reshape(1, d))


def kernel(x, mem, mem_norm_g, attn_norm_g, w_in_gla, w_gate_up, b_gate, gla_out_norm_g,
           w_in_swa, sinks, w_mem_kv, w_out, mlp_norm_g, w_up, w_down, final_norm_g):
    batch, seq, d = x.shape
    t = batch * seq
    xf = x.reshape(t, d)
    assert X_WIDTH == IN_TILE and SWA_KV_HEADS == 3

    mkv = _norm_matmul(mem.reshape(batch * MEM_LEN, d), mem_norm_g, w_mem_kv,
                       tm=batch * MEM_LEN)

    w_in_gla_t = jnp.swapaxes(w_in_gla, 1, 2)

    for i in range(DEPTH):
        j = i // 2
        if i % 2 == 0:
            proj = _in_proj(xf, attn_norm_g[i], w_in_gla_t, j, GLA_PIECES, GLA_CHUNKS,
                            transposed=True, tm=512)
            w_gate = jnp.pad(w_gate_up[j], ((0, GLA_RANK_PAD - GLA_GATE_RANK), (0, 0))).astype(BF16)
            o_mix = _gla_mixer(proj, w_gate, b_gate[j], gla_out_norm_g[j], batch, seq,
                               ts=1024, hpb=4)
            xq_block = (2 * GLA_V + 2 * GLA_K) // X_WIDTH
        else:
            proj = _in_proj(xf, attn_norm_g[i], w_in_swa, j, SWA_PIECES, SWA_CHUNKS,
                            transposed=False, tm=512)
            o_mix = _swa_mixer(proj, sinks[j], batch, seq, SWA_BLOCKS_PER_STEP)
            xq_block = SWA_Q // X_WIDTH
        o_mem = _mem_attention(proj, xq_block, mkv, i, batch, seq, tm=1024)
        xf = _out_proj(o_mix, o_mem, w_out, i, xf, tm=512)
        xf = _mlp(xf, mlp_norm_g[i], w_up, w_down, i, final_norm_g,
                  final_norm=(i == DEPTH - 1), tm=1024, tf=512)
    return xf.reshape(batch, seq, d)
```
